```python
import math
import jax
import jax.numpy as jnp
from jax import lax
import numpy as np

D_MODEL = 1024
BATCH = 8
SEQ = 4096
DEPTH = 4

HEAD_DIM = 64
N_EVEN = (DEPTH + 1) // 2
N_ODD = DEPTH // 2
NORM_EPS = 1e-6
D_FF = ((8 * D_MODEL // 3 + 255) // 256) * 256

POOL_WINDOWS = (2, 4, 8, 16)
POOL_GROUPS = len(POOL_WINDOWS)
POOL_WIDTH = D_MODEL // 4
POOL_GC = POOL_WIDTH // POOL_GROUPS

ATTN_Q_HEADS = (3 * D_MODEL // 4) // HEAD_DIM
ATTN_KV_HEADS = ATTN_Q_HEADS // 3
GQA_GROUP = ATTN_Q_HEADS // ATTN_KV_HEADS
WINDOW = 128
ATTN_BLOCK = 128
ROPE_THETA = 10000.0
EVEN_IN = POOL_WIDTH + (ATTN_Q_HEADS + 2 * ATTN_KV_HEADS) * HEAD_DIM
EVEN_MIX = POOL_WIDTH + ATTN_Q_HEADS * HEAD_DIM

CONV_CH = D_MODEL // 2
CONV_K = 31

SSM_INNER = D_MODEL
SSM_HEAD_DIM = 64
SSM_HEADS = SSM_INNER // SSM_HEAD_DIM
SSM_GROUPS = 4
SSM_STATE = 128
SSM_CONV = 4
SSM_CHUNK = 128
SSM_XBC = SSM_INNER + 2 * SSM_GROUPS * SSM_STATE
ODD_IN = 2 * CONV_CH + SSM_INNER + SSM_XBC + 2 * SSM_HEADS
ODD_MIX = CONV_CH + SSM_INNER

kernel_name = 'hybrid_pool_swa_conformer_ssd_encoder'


def rms_norm(x, w):
    xf = x.astype(jnp.float32)
    y = xf * lax.rsqrt(jnp.mean(xf * xf, axis=-1, keepdims=True) + NORM_EPS)
    return (y * w.astype(jnp.float32)).astype(x.dtype)


def layer_norm(x, g, b):
    xf = x.astype(jnp.float32)
    mu = jnp.mean(xf, axis=-1, keepdims=True)
    xc = xf - mu
    y = xc * lax.rsqrt(jnp.mean(xc * xc, axis=-1, keepdims=True) + NORM_EPS)
    return (y * g.astype(jnp.float32) + b.astype(jnp.float32)).astype(x.dtype)


def swiglu(h, w_gate, w_up, w_down):
    return (jax.nn.silu(h @ w_gate) * (h @ w_up)) @ w_down


def depthwise_conv(x, w, b, pad_left, pad_right):
    y = lax.conv_general_dilated(
        x, w[:, None, :].astype(x.dtype), window_strides=(1,),
        padding=[(pad_left, pad_right)],
        dimension_numbers=('NWC', 'WIO', 'NWC'),
        feature_group_count=x.shape[-1])
    return y + b.astype(x.dtype)


def rope_tables(seq_len):
    inv = 1.0 / (ROPE_THETA ** (jnp.arange(0, HEAD_DIM, 2, dtype=jnp.float32) / HEAD_DIM))
    ang = jnp.arange(seq_len, dtype=jnp.float32)[:, None] * inv[None, :]
    return jnp.cos(ang), jnp.sin(ang)


def apply_rope(t, cos, sin):
    tf = t.astype(jnp.float32)
    t1, t2 = tf[..., :HEAD_DIM // 2], tf[..., HEAD_DIM // 2:]
    c, s = cos[None, :, None, :], sin[None, :, None, :]
    return jnp.concatenate([t1 * c - t2 * s, t2 * c + t1 * s], axis=-1).astype(t.dtype)


def multiscale_pool(u, w_grp, scale):
    b, S, C = u.shape
    uf = u.astype(jnp.float32)
    cs = jnp.concatenate([jnp.zeros((b, 1, C), jnp.float32), jnp.cumsum(uf, axis=1)], axis=1)
    t = jnp.arange(S)
    outs = []
    for g, w in enumerate(POOL_WINDOWS):
        lo = jnp.clip(t - w // 2, 0, S)
        hi = jnp.clip(t + w - w // 2, 0, S)
        csg = cs[..., g * POOL_GC:(g + 1) * POOL_GC]
        cnt = (hi - lo).astype(jnp.float32)[None, :, None]
        mean = (jnp.take(csg, hi, axis=1) - jnp.take(csg, lo, axis=1)) / cnt
        outs.append(mean - uf[..., g * POOL_GC:(g + 1) * POOL_GC])
    p = jnp.stack(outs, axis=2).astype(u.dtype)
    y = jnp.einsum('bsgc,gcd->bsgd', p, w_grp).reshape(b, S, C)
    return y * scale


def windowed_gqa(q, k, v, sink):
    b, S = q.shape[0], q.shape[1]
    nb = S // ATTN_BLOCK
    qb = q.reshape(b, nb, ATTN_BLOCK, ATTN_KV_HEADS, GQA_GROUP, HEAD_DIM)

    def band(t):
        tp = jnp.pad(t, ((0, 0), (ATTN_BLOCK, ATTN_BLOCK), (0, 0), (0, 0)))
        tp = tp.reshape(b, nb + 2, ATTN_BLOCK, ATTN_KV_HEADS, HEAD_DIM)
        return jnp.concatenate([tp[:, :-2], tp[:, 1:-1], tp[:, 2:]], axis=2)

    kw, vw = band(k), band(v)
    s = jnp.einsum('bnqhgd,bnkhd->bnhgqk', qb, kw).astype(jnp.float32) * (HEAD_DIM ** -0.5)
    blk = jnp.arange(nb)[:, None, None] * ATTN_BLOCK
    qpos = blk + jnp.arange(ATTN_BLOCK)[None, :, None]
    kpos = blk - ATTN_BLOCK + jnp.arange(3 * ATTN_BLOCK)[None, None, :]
    valid = (jnp.abs(qpos - kpos) <= WINDOW) & (kpos >= 0) & (kpos < S)
    s = jnp.where(valid[None, :, None, None], s, -jnp.inf)
    sk = sink.astype(jnp.float32).reshape(ATTN_KV_HEADS, GQA_GROUP)[None, None, :, :, None, None]
    m = jnp.maximum(jnp.max(s, axis=-1, keepdims=True), sk)
    p = jnp.exp(s - m)
    denom = jnp.sum(p, axis=-1, keepdims=True) + jnp.exp(sk - m)
    o = jnp.einsum('bnhgqk,bnkhd->bnqhgd', (p / denom).astype(v.dtype), vw)
    return o.reshape(b, S, ATTN_Q_HEADS * HEAD_DIM)


def ssd_scan(x, dt, A, Bm, Cm):
    b, L, H, P = x.shape
    G, N = Bm.shape[2], Bm.shape[3]
    Hg = H // G
    c = L // SSM_CHUNK
    X = (x * dt[..., None]).reshape(b, c, SSM_CHUNK, G, Hg, P)
    a = jnp.moveaxis((dt * A).reshape(b, c, SSM_CHUNK, G, Hg), 2, -1)
    a_cs = jnp.cumsum(a, axis=-1)
    Bc = Bm.reshape(b, c, SSM_CHUNK, G, N)
    Cc = Cm.reshape(b, c, SSM_CHUNK, G, N)
    lower = jnp.tril(jnp.ones((SSM_CHUNK, SSM_CHUNK), bool))
    seg = a_cs[..., :, None] - a_cs[..., None, :]
    Lmat = jnp.exp(jnp.where(lower, seg, -jnp.inf))
    CB = jnp.einsum('bclgn,bcsgn->bcgls', Cc, Bc)
    y_diag = jnp.einsum('bcghls,bcsghp->bclghp', CB[:, :, :, None] * Lmat, X)
    decay_s = jnp.exp(a_cs[..., -1:] - a_cs)
    states = jnp.einsum('bclgn,bcghl,bclghp->bcghpn', Bc, decay_s, X)
    chunk_decay = jnp.exp(a_cs[..., -1])

    def step(h, inp):
        st, d = inp
        return h * d[..., None, None] + st, h

    h0 = jnp.zeros((b, G, Hg, P, N), jnp.float32)
    _, prev = lax.scan(step, h0, (jnp.moveaxis(states, 1, 0), jnp.moveaxis(chunk_decay, 1, 0)))
    prev = jnp.moveaxis(prev, 0, 1)
    y_off = jnp.einsum('bclgn,bcghpn,bcghl->bclghp', Cc, prev, jnp.exp(a_cs))
    return (y_diag + y_off).reshape(b, L, H, P)


def even_mixer(h, w_in, pool_w, pool_scale, sink, w_out, cos, sin):
    b, S, _ = h.shape
    proj = h @ w_in
    qw, kw = ATTN_Q_HEADS * HEAD_DIM, ATTN_KV_HEADS * HEAD_DIM
    o = POOL_WIDTH
    u = proj[..., :o]
    q = proj[..., o:o + qw].reshape(b, S, ATTN_Q_HEADS, HEAD_DIM)
    k = proj[..., o + qw:o + qw + kw].reshape(b, S, ATTN_KV_HEADS, HEAD_DIM)
    v = proj[..., o + qw + kw:o + qw + 2 * kw].reshape(b, S, ATTN_KV_HEADS, HEAD_DIM)
    q = apply_rope(q, cos, sin)
    k = apply_rope(k, cos, sin)
    pool_out = multiscale_pool(u, pool_w, pool_scale)
    attn_out = windowed_gqa(q, k, v, sink)
    return jnp.concatenate([pool_out, attn_out], axis=-1) @ w_out


def odd_mixer(h, w_in, dw_w, dw_b, ln_g, ln_b, conv_w, conv_b, A_log, dt_bias, D_skip, norm_w, w_out):
    b, S, _ = h.shape
    proj = h @ w_in
    glu_a = proj[..., :CONV_CH]
    glu_b = proj[..., CONV_CH:2 * CONV_CH]
    o = 2 * CONV_CH
    z = proj[..., o:o + SSM_INNER]
    o += SSM_INNER
    xbc = proj[..., o:o + SSM_XBC]
    o += SSM_XBC
    dt_raw = proj[..., o:o + 2 * SSM_HEADS].reshape(b, S, 2, SSM_HEADS)
    g = glu_a * jax.nn.sigmoid(glu_b)
    g = depthwise_conv(g, dw_w, dw_b, CONV_K // 2, CONV_K // 2)
    g = jax.nn.silu(layer_norm(g, ln_g, ln_b))
    xbc = jax.nn.silu(depthwise_conv(xbc, conv_w, conv_b, SSM_CONV // 2, SSM_CONV - 1 - SSM_CONV // 2))
    gn = SSM_GROUPS * SSM_STATE
    xs = xbc[..., :SSM_INNER].reshape(b, S, SSM_HEADS, SSM_HEAD_DIM).astype(jnp.float32)
    Bm = xbc[..., SSM_INNER:SSM_INNER + gn].reshape(b, S, SSM_GROUPS, SSM_STATE).astype(jnp.float32)
    Cm = xbc[..., SSM_INNER + gn:].reshape(b, S, SSM_GROUPS, SSM_STATE).astype(jnp.float32)
    dt = jax.nn.softplus(dt_raw.astype(jnp.float32) + dt_bias.astype(jnp.float32))
    A = -jnp.exp(A_log.astype(jnp.float32))
    flip = lambda t: jnp.flip(t, axis=1)
    y_f = ssd_scan(xs, dt[:, :, 0], A[0], Bm, Cm)
    y_b = flip(ssd_scan(flip(xs), flip(dt[:, :, 1]), A[1], flip(Bm), flip(Cm)))
    y = (y_f + y_b + xs * D_skip.astype(jnp.float32)[:, None]).reshape(b, S, SSM_INNER)
    y = rms_norm(y * jax.nn.silu(z.astype(jnp.float32)), norm_w).astype(h.dtype)
    return jnp.concatenate([g, y], axis=-1) @ w_out


def setup_inputs(seed: int = 0) -> dict:
    key = jax.random.key(seed)
    ks = jax.random.split(key, 24)
    f32 = jnp.float32

    def nrm(k, shape, fan_in):
        return jax.random.normal(k, shape, f32) * (fan_in ** -0.5)

    def near_one(k, shape, s):
        return 1.0 + s * jax.random.normal(k, shape, f32)

    dt0 = jnp.exp(jax.random.uniform(ks[20], (N_ODD, 2, SSM_HEADS), f32,
                                     minval=math.log(1e-3), maxval=math.log(1e-1)))
    return {
        'x': jax.random.normal(ks[0], (BATCH, SEQ, D_MODEL), f32),
        'norm_w': near_one(ks[1], (DEPTH, 6, D_MODEL), 0.05),
        'ffn_w_gate': nrm(ks[2], (DEPTH, 2, D_MODEL, D_FF), D_MODEL),
        'ffn_w_up': nrm(ks[3], (DEPTH, 2, D_MODEL, D_FF), D_MODEL),
        'ffn_w_down': nrm(ks[4], (DEPTH, 2, D_FF, D_MODEL), D_FF),
        'ev_w_in': nrm(ks[5], (N_EVEN, D_MODEL, EVEN_IN), D_MODEL),
        'ev_pool_w': nrm(ks[6], (N_EVEN, POOL_GROUPS, POOL_GC, POOL_GC), POOL_GC),
        'ev_pool_scale': near_one(ks[7], (N_EVEN, POOL_WIDTH), 0.1),
        'ev_sink': 0.5 * jax.random.normal(ks[8], (N_EVEN, ATTN_Q_HEADS), f32),
        'ev_w_out': nrm(ks[9], (N_EVEN, EVEN_MIX, D_MODEL), EVEN_MIX),
        'od_w_in': nrm(ks[10], (N_ODD, D_MODEL, ODD_IN), D_MODEL),
        'cv_dw_w': nrm(ks[11], (N_ODD, CONV_K, CONV_CH), CONV_K),
        'cv_dw_b': 0.02 * jax.random.normal(ks[12], (N_ODD, CONV_CH), f32),
        'cv_ln_g': near_one(ks[13], (N_ODD, CONV_CH), 0.05),
        'cv_ln_b': 0.02 * jax.random.normal(ks[14], (N_ODD, CONV_CH), f32),
        'ssm_conv_w': nrm(ks[15], (N_ODD, SSM_CONV, SSM_XBC), SSM_CONV),
        'ssm_conv_b': 0.02 * jax.random.normal(ks[16], (N_ODD, SSM_XBC), f32),
        'ssm_A_log': jnp.log(jax.random.uniform(ks[17], (N_ODD, 2, SSM_HEADS), f32, minval=1.0, maxval=16.0)),
        'ssm_dt_bias': dt0 + jnp.log(-jnp.expm1(-dt0)),
        'ssm_D': near_one(ks[18], (N_ODD, SSM_HEADS), 0.1),
        'ssm_norm_w': near_one(ks[19], (N_ODD, SSM_INNER), 0.05),
        'od_w_out': nrm(ks[21], (N_ODD, ODD_MIX, D_MODEL), ODD_MIX),
    }


def reference(x, norm_w, ffn_w_gate, ffn_w_up, ffn_w_down,
              ev_w_in, ev_pool_w, ev_pool_scale, ev_sink, ev_w_out,
              od_w_in, cv_dw_w, cv_dw_b, cv_ln_g, cv_ln_b,
              ssm_conv_w, ssm_conv_b, ssm_A_log, ssm_dt_bias, ssm_D, ssm_norm_w, od_w_out):
    cos, sin = rope_tables(x.shape[1])
    for l in range(DEPTH):
        nw = norm_w[l]
        x = x + 0.5 * rms_norm(swiglu(rms_norm(x, nw[0]), ffn_w_gate[l, 0], ffn_w_up[l, 0], ffn_w_down[l, 0]), nw[1])
        hn = rms_norm(x, nw[2])
        i = l // 2
        if l % 2 == 0:
            m = even_mixer(hn, ev_w_in[i], ev_pool_w[i], ev_pool_scale[i], ev_sink[i], ev_w_out[i], cos, sin)
        else:
            m = odd_mixer(hn, od_w_in[i], cv_dw_w[i], cv_dw_b[i], cv_ln_g[i], cv_ln_b[i],
                          ssm_conv_w[i], ssm_conv_b[i], ssm_A_log[i], ssm_dt_bias[i], ssm_D[i],
                          ssm_norm_w[i], od_w_out[i])
        x = x + rms_norm(m, nw[3])
        x = x + 0.5 * rms_norm(swiglu(rms_norm(x, nw[4]), ffn_w_gate[l, 1], ffn_w_up[l, 1], ffn_w_down[l, 1]), nw[5])
    return x
```

```python
import functools

import jax
import jax.numpy as jnp
from jax import lax
from jax.experimental import pallas as pl
from jax.experimental.pallas import tpu as pltpu

F32 = jnp.float32
MM_DTYPE = jnp.bfloat16
ACT_DTYPE = jnp.bfloat16

D_MODEL = 1024
DEPTH = 4
HEAD_DIM = 64
NORM_EPS = 1e-6
D_FF = 2816

POOL_WINDOWS = (2, 4, 8, 16)
POOL_WIDTH = 256
POOL_GC = 64
ATTN_Q_HEADS = 12
ATTN_KV_HEADS = 4
GQA_GROUP = 3
WINDOW = 128
ATTN_BLOCK = 128
ROPE_THETA = 10000.0
Q_WIDTH = ATTN_Q_HEADS * HEAD_DIM
KV_WIDTH = ATTN_KV_HEADS * HEAD_DIM
EVEN_IN = POOL_WIDTH + Q_WIDTH + 2 * KV_WIDTH

CONV_CH = 512
CONV_K = 31
SSM_INNER = 1024
SSM_HEAD_DIM = 64
SSM_HEADS = 16
SSM_GROUPS = 4
SSM_STATE = 128
SSM_CONV = 4
SSM_CHUNK = 128
SSM_GN = SSM_GROUPS * SSM_STATE
SSM_XBC = SSM_INNER + 2 * SSM_GN
GROUP_LANES = (SSM_HEADS // SSM_GROUPS) * SSM_HEAD_DIM

LANES = 128
ROW_TILE = 512
HALO = 16
POOL_HALO = 8
FF_CHUNK = 256
CONV_ROWS = 32
SSM_CONV_ROWS = 16
VMEM_LIMIT_BYTES = 56 * 1024 * 1024


def _params(*semantics):
    return pltpu.CompilerParams(dimension_semantics=semantics, vmem_limit_bytes=VMEM_LIMIT_BYTES)


def _rms(x, w):
    return x * lax.rsqrt(jnp.mean(x * x, axis=-1, keepdims=True) + NORM_EPS) * w


def _mm(a, b):
    return jnp.dot(a, b, preferred_element_type=F32)


def _mm_nt(a, b):
    return lax.dot_general(a, b, (((1,), (1,)), ((), ())), preferred_element_type=F32)


def _mm_tn(a, b):
    return lax.dot_general(a, b, (((0,), (0,)), ((), ())), preferred_element_type=F32)


def _mm_exact(a, b):
    return jnp.dot(a, b, preferred_element_type=F32, precision=lax.Precision.HIGHEST)


def _resident(shape):
    nd = len(shape)
    return pl.BlockSpec(shape, lambda *_: (0,) * nd, pipeline_mode=pl.Buffered(1))


def _rows(width, tile=ROW_TILE):
    return pl.BlockSpec((tile, width), lambda i: (i, 0))


def _ffn_body(x_ref, wpre_ref, wpost_ref, wg_ref, wu_ref, wd_ref, o_ref, acc_ref):
    x = x_ref[...]
    h = _rms(x, wpre_ref[...]).astype(MM_DTYPE)
    for c in range(D_FF // FF_CHUNK):
        cols = slice(c * FF_CHUNK, (c + 1) * FF_CHUNK)
        g = _mm(h, wg_ref[:, cols])
        u = _mm(h, wu_ref[:, cols])
        a = (g * jax.nn.sigmoid(g) * u).astype(MM_DTYPE)
        part = _mm(a, wd_ref[cols, :])
        if c == 0:
            acc_ref[...] = part
        else:
            acc_ref[...] += part
    o_ref[...] = x + 0.5 * _rms(acc_ref[...], wpost_ref[...])


def _ffn(x, w_pre, w_post, w_gate, w_up, w_down):
    rows = x.shape[0]
    return pl.pallas_call(
        _ffn_body,
        grid=(rows // ROW_TILE,),
        in_specs=[_rows(D_MODEL), _resident((1, D_MODEL)), _resident((1, D_MODEL)),
                  _resident((D_MODEL, D_FF)), _resident((D_MODEL, D_FF)), _resident((D_FF, D_MODEL))],
        out_specs=_rows(D_MODEL),
        out_shape=jax.ShapeDtypeStruct((rows, D_MODEL), F32),
        scratch_shapes=[pltpu.VMEM((ROW_TILE, D_MODEL), F32)],
        compiler_params=_params("parallel"),
        name="ffn",
    )(x, w_pre, w_post, w_gate, w_up, w_down)


def _even_in_body(x_ref, nw_ref, w_ref, cos_ref, sina_ref, sinb_ref, u_ref, q_ref, k_ref, v_ref):
    h = _rms(x_ref[...], nw_ref[...]).astype(MM_DTYPE)
    u_ref[...] = _mm(h, w_ref[:, :POOL_WIDTH])
    cos, sin_a, sin_b = cos_ref[...], sina_ref[...], sinb_ref[...]
    qk_width = Q_WIDTH + KV_WIDTH
    for j in range(qk_width // LANES):
        lo = POOL_WIDTH + j * LANES
        t = _mm(h, w_ref[:, lo:lo + LANES])
        r = t * cos + pltpu.roll(t, LANES - HEAD_DIM // 2, 1) * sin_a + pltpu.roll(t, HEAD_DIM // 2, 1) * sin_b
        if j * LANES < Q_WIDTH:
            q_ref[:, j * LANES:(j + 1) * LANES] = (r * (HEAD_DIM ** -0.5)).astype(q_ref.dtype)
        else:
            kc = j * LANES - Q_WIDTH
            k_ref[:, kc:kc + LANES] = r.astype(k_ref.dtype)
    v_ref[...] = _mm(h, w_ref[:, POOL_WIDTH + qk_width:]).astype(v_ref.dtype)


def _even_in(x, nw, w_in, cos, sin_a, sin_b, seq):
    rows = x.shape[0]
    tiles_per_seq = seq // ROW_TILE
    table = pl.BlockSpec((ROW_TILE, LANES), lambda i: (i % tiles_per_seq, 0))
    return pl.pallas_call(
        _even_in_body,
        grid=(rows // ROW_TILE,),
        in_specs=[_rows(D_MODEL), _resident((1, D_MODEL)), _resident((D_MODEL, EVEN_IN)), table, table, table],
        out_specs=[_rows(POOL_WIDTH), _rows(Q_WIDTH), _rows(KV_WIDTH), _rows(KV_WIDTH)],
        out_shape=[jax.ShapeDtypeStruct((rows, POOL_WIDTH), F32),
                   jax.ShapeDtypeStruct((rows, Q_WIDTH), ACT_DTYPE),
                   jax.ShapeDtypeStruct((rows, KV_WIDTH), ACT_DTYPE),
                   jax.ShapeDtypeStruct((rows, KV_WIDTH), ACT_DTYPE)],
        compiler_params=_params("parallel"),
        name="even_in",
    )(x, nw, w_in, cos, sin_a, sin_b)


def _attn_body(blocks_per_seq, sink_ref, q_ref, kp_ref, k_ref, kn_ref, vp_ref, v_ref, vn_ref, o_ref):
    blk = pl.program_id(0) % blocks_per_seq
    keys = jnp.concatenate([kp_ref[...], k_ref[...], kn_ref[...]], axis=0)
    vals = jnp.concatenate([vp_ref[...], v_ref[...], vn_ref[...]], axis=0)
    r = lax.broadcasted_iota(jnp.int32, (ATTN_BLOCK, 3 * ATTN_BLOCK), 0)
    c = lax.broadcasted_iota(jnp.int32, (ATTN_BLOCK, 3 * ATTN_BLOCK), 1)
    valid = (c - r >= 0) & (c - r <= 2 * WINDOW)
    valid = valid & ((c >= ATTN_BLOCK) | (blk > 0)) & ((c < 2 * ATTN_BLOCK) | (blk < blocks_per_seq - 1))
    q = q_ref[...]
    for kv in range(ATTN_KV_HEADS):
        kh = keys[:, kv * HEAD_DIM:(kv + 1) * HEAD_DIM]
        vh = vals[:, kv * HEAD_DIM:(kv + 1) * HEAD_DIM]
        for g in range(GQA_GROUP):
            hq = kv * GQA_GROUP + g
            s = _mm_nt(q[:, hq * HEAD_DIM:(hq + 1) * HEAD_DIM], kh)
            s = jnp.where(valid, s, -jnp.inf)
            sink = sink_ref[hq]
            m = jnp.maximum(jnp.max(s, axis=-1, keepdims=True), sink)
            p = jnp.exp(s - m)
            denom = jnp.sum(p, axis=-1, keepdims=True) + jnp.exp(sink - m)
            o = _mm((p / denom).astype(MM_DTYPE), vh)
            o_ref[:, hq * HEAD_DIM:(hq + 1) * HEAD_DIM] = o.astype(o_ref.dtype)


def _attn(q, k, v, sink, seq):
    rows = q.shape[0]
    nblk = rows // ATTN_BLOCK
    blocks_per_seq = seq // ATTN_BLOCK
    cur = lambda w: pl.BlockSpec((ATTN_BLOCK, w), lambda i: (i, 0))
    prev = pl.BlockSpec((ATTN_BLOCK, KV_WIDTH), lambda i: (jnp.maximum(i - 1, 0), 0))
    nxt = pl.BlockSpec((ATTN_BLOCK, KV_WIDTH), lambda i: (jnp.minimum(i + 1, nblk - 1), 0))
    return pl.pallas_call(
        functools.partial(_attn_body, blocks_per_seq),
        grid=(nblk,),
        in_specs=[pl.BlockSpec(memory_space=pltpu.SMEM), cur(Q_WIDTH),
                  prev, cur(KV_WIDTH), nxt, prev, cur(KV_WIDTH), nxt],
        out_specs=cur(Q_WIDTH),
        out_shape=jax.ShapeDtypeStruct((rows, Q_WIDTH), ACT_DTYPE),
        compiler_params=_params("parallel"),
        name="attn",
    )(sink, q, k, k, k, v, v, v)


def _even_out_body(tiles_per_seq, seq, x_ref, up_ref, u_ref, un_ref, a_ref, tap_ref, hw_ref, pw_ref, ps_ref,
                   wo_ref, nw_ref, o_ref, ue_ref):
    pos0 = (pl.program_id(0) % tiles_per_seq) * ROW_TILE
    ue_ref[0:POOL_HALO, :] = jnp.where(pos0 > 0, up_ref[...], 0.0)
    ue_ref[POOL_HALO:POOL_HALO + ROW_TILE, :] = u_ref[...]
    ue_ref[POOL_HALO + ROW_TILE:, :] = jnp.where(pos0 + ROW_TILE < seq, un_ref[...], 0.0)
    acc = jnp.zeros((ROW_TILE, POOL_WIDTH), F32)
    for d in range(2 * POOL_HALO):
        acc = acc + ue_ref[d:d + ROW_TILE, :] * tap_ref[d:d + 1, :]
    pos = pos0 + lax.broadcasted_iota(jnp.int32, (ROW_TILE, POOL_WIDTH), 0)
    hw = hw_ref[...]
    cnt = (jnp.minimum(pos + hw, seq) - jnp.maximum(pos - hw, 0)).astype(F32)
    p = (acc / cnt - u_ref[...]).astype(MM_DTYPE)
    y = (_mm(p, pw_ref[...]) * ps_ref[...]).astype(MM_DTYPE)
    m = _mm(y, wo_ref[:POOL_WIDTH, :]) + _mm(a_ref[...], wo_ref[POOL_WIDTH:, :])
    o_ref[...] = x_ref[...] + _rms(m, nw_ref[...])


def _even_out(x, u, attn, taps, half_w, pool_w, pool_scale, w_out, nw, seq):
    rows = x.shape[0]
    tiles_per_seq = seq // ROW_TILE
    per_tile = ROW_TILE // POOL_HALO
    last = rows // POOL_HALO - 1
    u_prev = pl.BlockSpec((POOL_HALO, POOL_WIDTH), lambda i: (jnp.maximum(i * per_tile - 1, 0), 0))
    u_next = pl.BlockSpec((POOL_HALO, POOL_WIDTH), lambda i: (jnp.minimum((i + 1) * per_tile, last), 0))
    return pl.pallas_call(
        functools.partial(_even_out_body, tiles_per_seq, seq),
        grid=(rows // ROW_TILE,),
        in_specs=[_rows(D_MODEL), u_prev, _rows(POOL_WIDTH), u_next, _rows(Q_WIDTH),
                  _resident((2 * POOL_HALO, POOL_WIDTH)), _resident((1, POOL_WIDTH)),
                  _resident((POOL_WIDTH, POOL_WIDTH)), _resident((1, POOL_WIDTH)),
                  _resident((POOL_WIDTH + Q_WIDTH, D_MODEL)), _resident((1, D_MODEL))],
        out_specs=_rows(D_MODEL),
        out_shape=jax.ShapeDtypeStruct((rows, D_MODEL), F32),
        scratch_shapes=[pltpu.VMEM((ROW_TILE + 2 * POOL_HALO, POOL_WIDTH), F32)],
        compiler_params=_params("parallel"),
        name="even_out",
    )(x, u, u, u, attn, taps, half_w, pool_w, pool_scale, w_out, nw)


def _rope_tables(seq):
    inv = 1.0 / (ROPE_THETA ** (jnp.arange(0, HEAD_DIM, 2, dtype=F32) / HEAD_DIM))
    ang = jnp.arange(seq, dtype=F32)[:, None] * inv[None, :]
    cos, sin = jnp.cos(ang), jnp.sin(ang)
    zero = jnp.zeros_like(sin)
    reps = LANES // HEAD_DIM
    cos_t = jnp.tile(jnp.concatenate([cos, cos], axis=1), (1, reps))
    sin_a = jnp.tile(jnp.concatenate([-sin, zero], axis=1), (1, reps))
    sin_b = jnp.tile(jnp.concatenate([zero, sin], axis=1), (1, reps))
    return cos_t, sin_a, sin_b


def _pool_constants():
    offs = jnp.arange(2 * POOL_HALO)[:, None] - POOL_HALO
    half = jnp.repeat(jnp.array([w // 2 for w in POOL_WINDOWS], jnp.int32), POOL_GC)[None, :]
    taps = ((offs >= -half) & (offs <= half - 1)).astype(F32)
    return taps, half


def _even_mixer(x, nw_pre, nw_post, w_in, pool_w, pool_scale, sink, w_out, seq):
    cos, sin_a, sin_b = _rope_tables(seq)
    taps, half_w = _pool_constants()
    u, q, k, v = _even_in(x, nw_pre, w_in.astype(MM_DTYPE), cos, sin_a, sin_b, seq)
    attn = _attn(q, k, v, sink, seq)
    pool_bd = jax.scipy.linalg.block_diag(*[pool_w[g] for g in range(len(POOL_WINDOWS))]).astype(MM_DTYPE)
    return _even_out(x, u, attn, taps, half_w, pool_bd, pool_scale[None, :], w_out.astype(MM_DTYPE), nw_post, seq)


def _odd_in_body(tiles_per_seq, seq, xp_ref, x_ref, xn_ref, nw_ref, wc_ref, wz_ref, wdt_ref,
                 dww_ref, dwb_ref, lng_ref, lnb_ref, cw_ref, cb_ref, dtb_ref,
                 g_ref, xbc_ref, z_ref, dt_ref, h_ref, ge_ref, xe_ref):
    ext = ROW_TILE + 2 * HALO
    pos0 = (pl.program_id(0) % tiles_per_seq) * ROW_TILE
    nw = nw_ref[...]
    h_ref[0:HALO, :] = _rms(xp_ref[...], nw).astype(MM_DTYPE)
    h_ref[HALO:HALO + ROW_TILE, :] = _rms(x_ref[...], nw).astype(MM_DTYPE)
    h_ref[HALO + ROW_TILE:, :] = _rms(xn_ref[...], nw).astype(MM_DTYPE)
    pos = pos0 - HALO + lax.broadcasted_iota(jnp.int32, (ext, 1), 0)
    inside = ((pos >= 0) & (pos < seq)).astype(F32)
    h_ext = h_ref[...]
    h_main = h_ref[HALO:HALO + ROW_TILE, :]

    glu_a = _mm(h_ext, wc_ref[:, :CONV_CH])
    glu_b = _mm(h_ext, wc_ref[:, CONV_CH:2 * CONV_CH])
    ge_ref[...] = glu_a * jax.nn.sigmoid(glu_b) * inside
    for j in range(SSM_XBC // CONV_CH):
        lo = 2 * CONV_CH + j * CONV_CH
        xe_ref[:, j * CONV_CH:(j + 1) * CONV_CH] = _mm(h_ext, wc_ref[:, lo:lo + CONV_CH]) * inside
    z_ref[...] = _mm(h_main, wz_ref[...]).astype(z_ref.dtype)
    dt_ref[...] = jax.nn.softplus(_mm(h_main, wdt_ref[...]) + dtb_ref[...])

    def conv_rows(i, carry):
        r0 = pl.multiple_of(i * CONV_ROWS, CONV_ROWS)
        acc = jnp.broadcast_to(dwb_ref[...], (CONV_ROWS, CONV_CH))
        win = ge_ref[pl.ds(r0, CONV_ROWS + 2 * HALO), :]
        for k in range(CONV_K):
            lo = HALO - CONV_K // 2 + k
            acc = acc + win[lo:lo + CONV_ROWS, :] * dww_ref[k:k + 1, :]
        mu = jnp.mean(acc, axis=-1, keepdims=True)
        xc = acc - mu
        y = xc * lax.rsqrt(jnp.mean(xc * xc, axis=-1, keepdims=True) + NORM_EPS) * lng_ref[...] + lnb_ref[...]
        g_ref[pl.ds(r0, CONV_ROWS), :] = (y * jax.nn.sigmoid(y)).astype(g_ref.dtype)
        return carry

    lax.fori_loop(0, ROW_TILE // CONV_ROWS, conv_rows, 0)

    def ssm_rows(i, carry):
        r0 = pl.multiple_of(i * SSM_CONV_ROWS, SSM_CONV_ROWS)
        for j in range(SSM_XBC // CONV_CH):
            cols = slice(j * CONV_CH, (j + 1) * CONV_CH)
            acc = jnp.broadcast_to(cb_ref[:, cols], (SSM_CONV_ROWS, CONV_CH))
            win = xe_ref[pl.ds(r0 + HALO - 8, SSM_CONV_ROWS + 16), cols]
            for k in range(SSM_CONV):
                lo = 8 - SSM_CONV // 2 + k
                acc = acc + win[lo:lo + SSM_CONV_ROWS, :] * cw_ref[k:k + 1, cols]
            xbc_ref[pl.ds(r0, SSM_CONV_ROWS), cols] = (acc * jax.nn.sigmoid(acc)).astype(xbc_ref.dtype)
        return carry

    lax.fori_loop(0, ROW_TILE // SSM_CONV_ROWS, ssm_rows, 0)


def _odd_in(x, nw, w_conv, w_z, w_dt, dw_w, dw_b, ln_g, ln_b, conv_w, conv_b, dt_bias, seq):
    rows = x.shape[0]
    tiles_per_seq = seq // ROW_TILE
    per_tile = ROW_TILE // HALO
    last = rows // HALO - 1
    x_prev = pl.BlockSpec((HALO, D_MODEL), lambda i: (jnp.maximum(i * per_tile - 1, 0), 0))
    x_next = pl.BlockSpec((HALO, D_MODEL), lambda i: (jnp.minimum((i + 1) * per_tile, last), 0))
    ext = ROW_TILE + 2 * HALO
    return pl.pallas_call(
        functools.partial(_odd_in_body, tiles_per_seq, seq),
        grid=(rows // ROW_TILE,),
        in_specs=[x_prev, _rows(D_MODEL), x_next, _resident((1, D_MODEL)),
                  _resident((D_MODEL, 2 * CONV_CH + SSM_XBC)), _resident((D_MODEL, SSM_INNER)),
                  _resident((D_MODEL, LANES)),
                  _resident((CONV_K, CONV_CH)), _resident((1, CONV_CH)), _resident((1, CONV_CH)),
                  _resident((1, CONV_CH)), _resident((SSM_CONV, SSM_XBC)), _resident((1, SSM_XBC)),
                  _resident((1, LANES))],
        out_specs=[_rows(CONV_CH), _rows(SSM_XBC), _rows(SSM_INNER), _rows(LANES)],
        out_shape=[jax.ShapeDtypeStruct((rows, CONV_CH), ACT_DTYPE),
                   jax.ShapeDtypeStruct((rows, SSM_XBC), ACT_DTYPE),
                   jax.ShapeDtypeStruct((rows, SSM_INNER), F32),
                   jax.ShapeDtypeStruct((rows, LANES), F32)],
        scratch_shapes=[pltpu.VMEM((ext, D_MODEL), MM_DTYPE),
                        pltpu.VMEM((ext, CONV_CH), F32),
                        pltpu.VMEM((ext, SSM_XBC), F32)],
        compiler_params=_params("parallel"),
        name="odd_in",
    )(x, x, x, nw, w_conv, w_z, w_dt, dw_w, dw_b, ln_g, ln_b, conv_w, conv_b, dt_bias)


def _pair_cols(v, h0):
    lane = lax.broadcasted_iota(jnp.int32, (SSM_CHUNK, LANES), 1)
    return jnp.where(lane < SSM_HEAD_DIM, v[:, h0:h0 + 1], v[:, h0 + 1:h0 + 2])


def _ssd_direction(xbc, dt, alog_row, alog_col, col0, state_ref, reverse):
    xs = xbc[:, :SSM_INNER].astype(F32)
    bm = xbc[:, SSM_INNER:SSM_INNER + SSM_GN]
    cm = xbc[:, SSM_INNER + SSM_GN:]
    li = lax.broadcasted_iota(jnp.int32, (SSM_CHUNK, SSM_CHUNK), 0)
    si = lax.broadcasted_iota(jnp.int32, (SSM_CHUNK, SSM_CHUNK), 1)
    if reverse:
        keep = si >= li
        sum_rows = (si >= li).astype(F32)
        sum_lanes = (li >= si).astype(F32)
        edge = 0
    else:
        keep = si <= li
        sum_rows = (si <= li).astype(F32)
        sum_lanes = (li <= si).astype(F32)
        edge = SSM_CHUNK - 1
    a_col = dt * (-jnp.exp(alog_row))
    a_row = dt.T * (-jnp.exp(alog_col))
    acs_col = _mm_exact(sum_rows, a_col)
    acs_row = _mm_exact(a_row, sum_lanes)
    total = acs_col[edge:edge + 1, :]
    in_decay = jnp.exp(acs_col)
    out_decay = jnp.exp(total - acs_col)
    chunk_decay = jnp.exp(total)

    y_parts = []
    xd_parts = []
    for g in range(SSM_GROUPS):
        bg = bm[:, g * SSM_STATE:(g + 1) * SSM_STATE]
        cg = cm[:, g * SSM_STATE:(g + 1) * SSM_STATE]
        cb = _mm_nt(cg, bg)
        y_off = _mm(cg, state_ref[g].astype(MM_DTYPE))
        for pair in range(GROUP_LANES // LANES):
            h0 = g * (SSM_HEADS // SSM_GROUPS) + 2 * pair
            lo = h0 * SSM_HEAD_DIM
            x_pair = xs[:, lo:lo + LANES] * _pair_cols(dt, col0 + h0)
            xp = x_pair.astype(MM_DTYPE)
            halves = []
            for h in (h0, h0 + 1):
                seg = acs_col[:, col0 + h:col0 + h + 1] - acs_row[col0 + h:col0 + h + 1, :]
                lmat = jnp.exp(jnp.where(keep, seg, -jnp.inf))
                halves.append(_mm((cb * lmat).astype(MM_DTYPE), xp))
            lane = lax.broadcasted_iota(jnp.int32, (SSM_CHUNK, LANES), 1)
            y_diag = jnp.where(lane < SSM_HEAD_DIM, halves[0], halves[1])
            y_parts.append(y_diag + y_off[:, pair * LANES:(pair + 1) * LANES] * _pair_cols(in_decay, col0 + h0))
            xd_parts.append((x_pair * _pair_cols(out_decay, col0 + h0)).astype(MM_DTYPE))
        xd = jnp.concatenate(xd_parts[-(GROUP_LANES // LANES):], axis=1)
        new_state = _mm_tn(bg, xd)
        cd = jnp.concatenate(
            [jnp.broadcast_to(chunk_decay[:, col0 + g * 4 + j:col0 + g * 4 + j + 1], (1, SSM_HEAD_DIM))
             for j in range(SSM_HEADS // SSM_GROUPS)], axis=1)
        state_ref[g] = state_ref[g] * cd + new_state
    return jnp.concatenate(y_parts, axis=1)


def _ssd_body(nchunks, xf_ref, xb_ref, dtf_ref, dtb_ref, alr_ref, alc_ref, yf_ref, yb_ref, sf_ref, sb_ref):
    @pl.when(pl.program_id(1) == 0)
    def _():
        sf_ref[...] = jnp.zeros_like(sf_ref)
        sb_ref[...] = jnp.zeros_like(sb_ref)

    alr, alc = alr_ref[...], alc_ref[...]
    yf_ref[...] = _ssd_direction(xf_ref[...], dtf_ref[...], alr, alc, 0, sf_ref, False)
    yb_ref[...] = _ssd_direction(xb_ref[...], dtb_ref[...], alr, alc, SSM_HEADS, sb_ref, True)


def _ssd(xbc, dt, alog_row, alog_col, batch, seq):
    rows = xbc.shape[0]
    nchunks = seq // SSM_CHUNK
    fwd = lambda w: pl.BlockSpec((SSM_CHUNK, w), lambda b, i: (b * nchunks + i, 0))
    bwd = lambda w: pl.BlockSpec((SSM_CHUNK, w), lambda b, i: (b * nchunks + nchunks - 1 - i, 0))
    state = pltpu.VMEM((SSM_GROUPS, SSM_STATE, GROUP_LANES), F32)
    return pl.pallas_call(
        functools.partial(_ssd_body, nchunks),
        grid=(batch, nchunks),
        in_specs=[fwd(SSM_XBC), bwd(SSM_XBC), fwd(LANES), bwd(LANES),
                  pl.BlockSpec((1, LANES), lambda b, i: (0, 0)), pl.BlockSpec((LANES, 1), lambda b, i: (0, 0))],
        out_specs=[fwd(SSM_INNER), bwd(SSM_INNER)],
        out_shape=[jax.ShapeDtypeStruct((rows, SSM_INNER), F32), jax.ShapeDtypeStruct((rows, SSM_INNER), F32)],
        scratch_shapes=[state, state],
        compiler_params=_params("parallel", "arbitrary"),
        name="ssd",
    )(xbc, xbc, dt, dt, alog_row, alog_col)


def _odd_out_body(x_ref, g_ref, xs_ref, yf_ref, yb_ref, z_ref, dskip_ref, gnw_ref, wo_ref, nw_ref, o_ref):
    z = z_ref[...].astype(F32)
    y = yf_ref[...] + yb_ref[...] + xs_ref[...].astype(F32) * dskip_ref[...]
    y = _rms(y * (z * jax.nn.sigmoid(z)), gnw_ref[...]).astype(MM_DTYPE)
    m = _mm(g_ref[...], wo_ref[:CONV_CH, :]) + _mm(y, wo_ref[CONV_CH:, :])
    o_ref[...] = x_ref[...] + _rms(m, nw_ref[...])


def _odd_out(x, g, xbc, y_f, y_b, z, d_skip, gn_w, w_out, nw):
    rows = x.shape[0]
    return pl.pallas_call(
        _odd_out_body,
        grid=(rows // ROW_TILE,),
        in_specs=[_rows(D_MODEL), _rows(CONV_CH), _rows(SSM_INNER), _rows(SSM_INNER), _rows(SSM_INNER),
                  _rows(SSM_INNER), _resident((1, SSM_INNER)), _resident((1, SSM_INNER)),
                  _resident((CONV_CH + SSM_INNER, D_MODEL)), _resident((1, D_MODEL))],
        out_specs=_rows(D_MODEL),
        out_shape=jax.ShapeDtypeStruct((rows, D_MODEL), F32),
        compiler_params=_params("parallel"),
        name="odd_out",
    )(x, g, xbc, y_f, y_b, z, d_skip, gn_w, w_out, nw)


def _odd_mixer(x, nw_pre, nw_post, w_in, dw_w, dw_b, ln_g, ln_b, conv_w, conv_b, a_log, dt_bias, d_skip,
               gn_w, w_out, batch, seq):
    n_conv = 2 * CONV_CH
    off_z, off_xbc, off_dt = n_conv, n_conv + SSM_INNER, n_conv + SSM_INNER + SSM_XBC
    w_conv = jnp.concatenate([w_in[:, :n_conv], w_in[:, off_xbc:off_dt]], axis=1).astype(MM_DTYPE)
    w_z = w_in[:, off_z:off_xbc].astype(MM_DTYPE)
    n_dt = 2 * SSM_HEADS
    w_dt = jnp.pad(w_in[:, off_dt:], ((0, 0), (0, LANES - n_dt))).astype(MM_DTYPE)
    dtb = jnp.pad(dt_bias.reshape(1, n_dt), ((0, 0), (0, LANES - n_dt)))
    alog = jnp.pad(a_log.reshape(1, n_dt), ((0, 0), (0, LANES - n_dt)))
    g, xbc, z, dt = _odd_in(x, nw_pre, w_conv, w_z, w_dt, dw_w, dw_b[None, :], ln_g[None, :], ln_b[None, :],
                            conv_w, conv_b[None, :], dtb, seq)
    y_f, y_b = _ssd(xbc, dt, alog, alog.reshape(LANES, 1), batch, seq)
    d_row = jnp.repeat(d_skip, SSM_HEAD_DIM)[None, :]
    return _odd_out(x, g, xbc, y_f, y_b, z, d_row, gn_w[None, :], w_out.astype(MM_DTYPE), nw_post)


def kernel(x, norm_w, ffn_w_gate, ffn_w_up, ffn_w_down, ev_w_in, ev_pool_w, ev_pool_scale, ev_sink, ev_w_out,
           od_w_in, cv_dw_w, cv_dw_b, cv_ln_g, cv_ln_b, ssm_conv_w, ssm_conv_b, ssm_A_log, ssm_dt_bias, ssm_D,
           ssm_norm_w, od_w_out):
    batch, seq, d = x.shape
    assert d == D_MODEL and seq % ROW_TILE == 0
    h = x.reshape(batch * seq, d)
    wg, wu, wd = (w.astype(MM_DTYPE) for w in (ffn_w_gate, ffn_w_up, ffn_w_down))
    for l in range(DEPTH):
        nw = norm_w[l][:, None, :]
        i = l // 2
        h = _ffn(h, nw[0], nw[1], wg[l, 0], wu[l, 0], wd[l, 0])
        if l % 2 == 0:
            h = _even_mixer(h, nw[2], nw[3], ev_w_in[i], ev_pool_w[i], ev_pool_scale[i], ev_sink[i],
                            ev_w_out[i], seq)
        else:
            h = _odd_mixer(h, nw[2], nw[3], od_w_in[i], cv_dw_w[i], cv_dw_b[i], cv_ln_g[i], cv_ln_b[i],
                           ssm_conv_w[i], ssm_conv_b[i], ssm_A_log[i], ssm_dt_bias[i], ssm_D[i],
                           ssm_norm_w[i], od_w_out[i], batch, seq)
        h = _ffn(h, nw[4], nw[5], wg[l, 1], wu[l, 1], wd[l, 1])
    return h.reshape(batch, seq, d)
```

```python
import functools

import jax
import jax.numpy as jnp
from jax import lax
from jax.experimental import pallas as pl
from jax.experimental.pallas import tpu as pltpu

F32 = jnp.float32
MM_DTYPE = jnp.bfloat16
ACT_DTYPE = jnp.bfloat16

D_MODEL = 1024
DEPTH = 4
HEAD_DIM = 64
NORM_EPS = 1e-6
D_FF = 2816

POOL_WINDOWS = (2, 4, 8, 16)
POOL_WIDTH = 256
POOL_GC = 64
ATTN_Q_HEADS = 12
ATTN_KV_HEADS = 4
GQA_GROUP = 3
WINDOW = 128
ATTN_BLOCK = 128
ROPE_THETA = 10000.0
Q_WIDTH = ATTN_Q_HEADS * HEAD_DIM
KV_WIDTH = ATTN_KV_HEADS * HEAD_DIM
EVEN_IN = POOL_WIDTH + Q_WIDTH + 2 * KV_WIDTH

CONV_CH = 512
CONV_K = 31
SSM_INNER = 1024
SSM_HEAD_DIM = 64
SSM_HEADS = 16
SSM_GROUPS = 4
SSM_STATE = 128
SSM_CONV = 4
SSM_CHUNK = 128
SSM_GN = SSM_GROUPS * SSM_STATE
SSM_XBC = SSM_INNER + 2 * SSM_GN
GROUP_LANES = (SSM_HEADS // SSM_GROUPS) * SSM_HEAD_DIM

LANES = 128
SUBLANES = 8
ROW_TILE = 512
HALO = 16
POOL_HALO = 8
ATTN_TILE = 256
FF_CHUNK = 256
CONV_ROWS = 128
SSM_CONV_ROWS = 64
SSM_CONV_LANES = 256
VMEM_LIMIT_BYTES = 56 * 1024 * 1024


def _params(*semantics):
    return pltpu.CompilerParams(dimension_semantics=semantics, vmem_limit_bytes=VMEM_LIMIT_BYTES)


def _rms(x, w):
    return x * lax.rsqrt(jnp.mean(x * x, axis=-1, keepdims=True) + NORM_EPS) * w


def _mm(a, b):
    return jnp.dot(a, b, preferred_element_type=F32)


def _mm_nt(a, b):
    return lax.dot_general(a, b, (((1,), (1,)), ((), ())), preferred_element_type=F32)


def _mm_tn(a, b):
    return lax.dot_general(a, b, (((0,), (0,)), ((), ())), preferred_element_type=F32)


def _mm_exact(a, b):
    return jnp.dot(a, b, preferred_element_type=F32, precision=lax.Precision.HIGHEST)


def _rows_up(v, r):
    return v if r == 0 else pltpu.roll(v, v.shape[0] - r, 0)


def _resident(shape):
    nd = len(shape)
    return pl.BlockSpec(shape, lambda *_: (0,) * nd, pipeline_mode=pl.Buffered(1))


def _rows(width, tile=ROW_TILE):
    return pl.BlockSpec((tile, width), lambda i: (i, 0))


def _ffn_body(x_ref, wpre_ref, wpost_ref, wg_ref, wu_ref, wd_ref, o_ref, acc_ref):
    x = x_ref[...]
    h = _rms(x, wpre_ref[...]).astype(MM_DTYPE)
    for c in range(D_FF // FF_CHUNK):
        cols = slice(c * FF_CHUNK, (c + 1) * FF_CHUNK)
        g = _mm(h, wg_ref[:, cols])
        u = _mm(h, wu_ref[:, cols])
        a = (g * jax.nn.sigmoid(g) * u).astype(MM_DTYPE)
        part = _mm(a, wd_ref[cols, :])
        if c == 0:
            acc_ref[...] = part
        else:
            acc_ref[...] += part
    o_ref[...] = x + 0.5 * _rms(acc_ref[...], wpost_ref[...])


def _ffn(x, w_pre, w_post, w_gate, w_up, w_down):
    rows = x.shape[0]
    return pl.pallas_call(
        _ffn_body,
        grid=(rows // ROW_TILE,),
        in_specs=[_rows(D_MODEL), _resident((1, D_MODEL)), _resident((1, D_MODEL)),
                  _resident((D_MODEL, D_FF)), _resident((D_MODEL, D_FF)), _resident((D_FF, D_MODEL))],
        out_specs=_rows(D_MODEL),
        out_shape=jax.ShapeDtypeStruct((rows, D_MODEL), F32),
        scratch_shapes=[pltpu.VMEM((ROW_TILE, D_MODEL), F32)],
        compiler_params=_params("parallel"),
        name="ffn",
    )(x, w_pre, w_post, w_gate, w_up, w_down)


def _even_in_body(x_ref, nw_ref, w_ref, cos_ref, sina_ref, sinb_ref, u_ref, q_ref, k_ref, v_ref):
    h = _rms(x_ref[...], nw_ref[...]).astype(MM_DTYPE)
    u_ref[...] = _mm(h, w_ref[:, :POOL_WIDTH])
    cos, sin_a, sin_b = cos_ref[...], sina_ref[...], sinb_ref[...]
    qk_width = Q_WIDTH + KV_WIDTH
    for j in range(qk_width // LANES):
        lo = POOL_WIDTH + j * LANES
        t = _mm(h, w_ref[:, lo:lo + LANES])
        r = t * cos + pltpu.roll(t, LANES - HEAD_DIM // 2, 1) * sin_a + pltpu.roll(t, HEAD_DIM // 2, 1) * sin_b
        if j * LANES < Q_WIDTH:
            q_ref[:, j * LANES:(j + 1) * LANES] = (r * (HEAD_DIM ** -0.5)).astype(q_ref.dtype)
        else:
            kc = j * LANES - Q_WIDTH
            k_ref[:, kc:kc + LANES] = r.astype(k_ref.dtype)
    v_ref[...] = _mm(h, w_ref[:, POOL_WIDTH + qk_width:]).astype(v_ref.dtype)


def _even_in(x, nw, w_in, cos, sin_a, sin_b, seq):
    rows = x.shape[0]
    tiles_per_seq = seq // ROW_TILE
    table = pl.BlockSpec((ROW_TILE, LANES), lambda i: (i % tiles_per_seq, 0))
    return pl.pallas_call(
        _even_in_body,
        grid=(rows // ROW_TILE,),
        in_specs=[_rows(D_MODEL), _resident((1, D_MODEL)), _resident((D_MODEL, EVEN_IN)), table, table, table],
        out_specs=[_rows(POOL_WIDTH), _rows(Q_WIDTH), _rows(KV_WIDTH), _rows(KV_WIDTH)],
        out_shape=[jax.ShapeDtypeStruct((rows, POOL_WIDTH), F32),
                   jax.ShapeDtypeStruct((rows, Q_WIDTH), ACT_DTYPE),
                   jax.ShapeDtypeStruct((rows, KV_WIDTH), ACT_DTYPE),
                   jax.ShapeDtypeStruct((rows, KV_WIDTH), ACT_DTYPE)],
        compiler_params=_params("parallel"),
        name="even_in",
    )(x, nw, w_in, cos, sin_a, sin_b)


def _attn_body(blocks_per_seq, sink_ref, q_ref, kp_ref, k_ref, kn_ref, vp_ref, v_ref, vn_ref, o_ref):
    blk0 = (pl.program_id(0) * (ATTN_TILE // ATTN_BLOCK)) % blocks_per_seq
    keys = jnp.concatenate([kp_ref[...], k_ref[...], kn_ref[...]], axis=0)
    vals = jnp.concatenate([vp_ref[...], v_ref[...], vn_ref[...]], axis=0)
    vals_t = vals.astype(F32).T.astype(MM_DTYPE)
    nwin = 3 * ATTN_BLOCK
    nq = GQA_GROUP * ATTN_BLOCK
    head_of_lane = lax.broadcasted_iota(jnp.int32, (nq, KV_WIDTH), 1) // HEAD_DIM
    key_i = lax.broadcasted_iota(jnp.int32, (nwin, ATTN_BLOCK), 0)
    qry_i = lax.broadcasted_iota(jnp.int32, (nwin, ATTN_BLOCK), 1)
    band = (key_i - qry_i >= 0) & (key_i - qry_i <= 2 * WINDOW)
    for jb in range(ATTN_TILE // ATTN_BLOCK):
        blk = blk0 + jb
        rows = slice(jb * ATTN_BLOCK, (jb + 1) * ATTN_BLOCK)
        win = slice(jb * ATTN_BLOCK, jb * ATTN_BLOCK + nwin)
        valid = band & ((key_i >= ATTN_BLOCK) | (blk > 0)) & ((key_i < 2 * ATTN_BLOCK) | (blk < blocks_per_seq - 1))
        bias = jnp.where(valid, 0.0, -jnp.inf)
        bias = jnp.concatenate([bias] * GQA_GROUP, axis=1)
        qs = jnp.concatenate([q_ref[rows, g * KV_WIDTH:(g + 1) * KV_WIDTH] for g in range(GQA_GROUP)], axis=0)
        qm = jnp.concatenate([jnp.where(head_of_lane == kv, qs, jnp.zeros_like(qs))
                              for kv in range(ATTN_KV_HEADS)], axis=0)
        scores = _mm_nt(keys[win], qm)
        outs = []
        for kv in range(ATTN_KV_HEADS):
            s = scores[:, kv * nq:(kv + 1) * nq] + bias
            sink = jnp.concatenate([jnp.full((1, ATTN_BLOCK), sink_ref[kv * GQA_GROUP + g], F32)
                                    for g in range(GQA_GROUP)], axis=1)
            m = jnp.maximum(jnp.max(s, axis=0, keepdims=True), sink)
            p = jnp.exp(s - m)
            denom = jnp.sum(p, axis=0, keepdims=True) + jnp.exp(sink - m)
            pn = (p * (1.0 / denom)).astype(MM_DTYPE)
            outs.append(_mm(vals_t[kv * HEAD_DIM:(kv + 1) * HEAD_DIM, win], pn))
        out = jnp.concatenate(outs, axis=0).T
        for g in range(GQA_GROUP):
            o_ref[rows, g * KV_WIDTH:(g + 1) * KV_WIDTH] = out[g * ATTN_BLOCK:(g + 1) * ATTN_BLOCK].astype(o_ref.dtype)


def _attn(q, k, v, sink, seq):
    rows = q.shape[0]
    per_tile = ATTN_TILE // ATTN_BLOCK
    last = rows // ATTN_BLOCK - 1
    blocks_per_seq = seq // ATTN_BLOCK
    cur = lambda w: pl.BlockSpec((ATTN_TILE, w), lambda i: (i, 0))
    prev = pl.BlockSpec((ATTN_BLOCK, KV_WIDTH), lambda i: (jnp.maximum(i * per_tile - 1, 0), 0))
    nxt = pl.BlockSpec((ATTN_BLOCK, KV_WIDTH), lambda i: (jnp.minimum((i + 1) * per_tile, last), 0))
    return pl.pallas_call(
        functools.partial(_attn_body, blocks_per_seq),
        grid=(rows // ATTN_TILE,),
        in_specs=[pl.BlockSpec(memory_space=pltpu.SMEM), cur(Q_WIDTH),
                  prev, cur(KV_WIDTH), nxt, prev, cur(KV_WIDTH), nxt],
        out_specs=cur(Q_WIDTH),
        out_shape=jax.ShapeDtypeStruct((rows, Q_WIDTH), ACT_DTYPE),
        compiler_params=_params("parallel"),
        name="attn",
    )(sink, q, k, k, k, v, v, v)


def _even_out_body(tiles_per_seq, seq, x_ref, up_ref, u_ref, un_ref, a_ref, tap_ref, hw_ref, pw_ref, ps_ref,
                   wo_ref, nw_ref, o_ref, ue_ref):
    pos0 = (pl.program_id(0) % tiles_per_seq) * ROW_TILE
    ue_ref[0:POOL_HALO, :] = jnp.where(pos0 > 0, up_ref[...], 0.0)
    ue_ref[POOL_HALO:POOL_HALO + ROW_TILE, :] = u_ref[...]
    ue_ref[POOL_HALO + ROW_TILE:, :] = jnp.where(pos0 + ROW_TILE < seq, un_ref[...], 0.0)
    acc = jnp.zeros((ROW_TILE, POOL_WIDTH), F32)
    for d in range(2 * POOL_HALO):
        acc = acc + ue_ref[d:d + ROW_TILE, :] * tap_ref[d:d + 1, :]
    pos = pos0 + lax.broadcasted_iota(jnp.int32, (ROW_TILE, POOL_WIDTH), 0)
    hw = hw_ref[...]
    cnt = (jnp.minimum(pos + hw, seq) - jnp.maximum(pos - hw, 0)).astype(F32)
    p = (acc / cnt - u_ref[...]).astype(MM_DTYPE)
    y = (_mm(p, pw_ref[...]) * ps_ref[...]).astype(MM_DTYPE)
    m = _mm(y, wo_ref[:POOL_WIDTH, :]) + _mm(a_ref[...], wo_ref[POOL_WIDTH:, :])
    o_ref[...] = x_ref[...] + _rms(m, nw_ref[...])


def _even_out(x, u, attn, taps, half_w, pool_w, pool_scale, w_out, nw, seq):
    rows = x.shape[0]
    tiles_per_seq = seq // ROW_TILE
    per_tile = ROW_TILE // POOL_HALO
    last = rows // POOL_HALO - 1
    u_prev = pl.BlockSpec((POOL_HALO, POOL_WIDTH), lambda i: (jnp.maximum(i * per_tile - 1, 0), 0))
    u_next = pl.BlockSpec((POOL_HALO, POOL_WIDTH), lambda i: (jnp.minimum((i + 1) * per_tile, last), 0))
    return pl.pallas_call(
        functools.partial(_even_out_body, tiles_per_seq, seq),
        grid=(rows // ROW_TILE,),
        in_specs=[_rows(D_MODEL), u_prev, _rows(POOL_WIDTH), u_next, _rows(Q_WIDTH),
                  _resident((2 * POOL_HALO, POOL_WIDTH)), _resident((1, POOL_WIDTH)),
                  _resident((POOL_WIDTH, POOL_WIDTH)), _resident((1, POOL_WIDTH)),
                  _resident((POOL_WIDTH + Q_WIDTH, D_MODEL)), _resident((1, D_MODEL))],
        out_specs=_rows(D_MODEL),
        out_shape=jax.ShapeDtypeStruct((rows, D_MODEL), F32),
        scratch_shapes=[pltpu.VMEM((ROW_TILE + 2 * POOL_HALO, POOL_WIDTH), F32)],
        compiler_params=_params("parallel"),
        name="even_out",
    )(x, u, u, u, attn, taps, half_w, pool_w, pool_scale, w_out, nw)


def _rope_tables(seq):
    inv = 1.0 / (ROPE_THETA ** (jnp.arange(0, HEAD_DIM, 2, dtype=F32) / HEAD_DIM))
    ang = jnp.arange(seq, dtype=F32)[:, None] * inv[None, :]
    cos, sin = jnp.cos(ang), jnp.sin(ang)
    zero = jnp.zeros_like(sin)
    reps = LANES // HEAD_DIM
    cos_t = jnp.tile(jnp.concatenate([cos, cos], axis=1), (1, reps))
    sin_a = jnp.tile(jnp.concatenate([-sin, zero], axis=1), (1, reps))
    sin_b = jnp.tile(jnp.concatenate([zero, sin], axis=1), (1, reps))
    return cos_t, sin_a, sin_b


def _pool_constants():
    offs = jnp.arange(2 * POOL_HALO)[:, None] - POOL_HALO
    half = jnp.repeat(jnp.array([w // 2 for w in POOL_WINDOWS], jnp.int32), POOL_GC)[None, :]
    taps = ((offs >= -half) & (offs <= half - 1)).astype(F32)
    return taps, half


def _even_mixer(x, nw_pre, nw_post, w_in, pool_w, pool_scale, sink, w_out, seq):
    cos, sin_a, sin_b = _rope_tables(seq)
    taps, half_w = _pool_constants()
    head_order = jnp.array([kv * GQA_GROUP + g for g in range(GQA_GROUP) for kv in range(ATTN_KV_HEADS)])
    q_cols = (head_order[:, None] * HEAD_DIM + jnp.arange(HEAD_DIM)[None, :]).reshape(-1)
    cols = jnp.concatenate([jnp.arange(POOL_WIDTH), POOL_WIDTH + q_cols, jnp.arange(POOL_WIDTH + Q_WIDTH, EVEN_IN)])
    u, q, k, v = _even_in(x, nw_pre, w_in[:, cols].astype(MM_DTYPE), cos, sin_a, sin_b, seq)
    attn = _attn(q, k, v, sink, seq)
    pool_bd = jax.scipy.linalg.block_diag(*[pool_w[g] for g in range(len(POOL_WINDOWS))]).astype(MM_DTYPE)
    w_out_rows = jnp.concatenate([jnp.arange(POOL_WIDTH), POOL_WIDTH + q_cols])
    return _even_out(x, u, attn, taps, half_w, pool_bd, pool_scale[None, :], w_out[w_out_rows].astype(MM_DTYPE),
                     nw_post, seq)


def _odd_in_body(tiles_per_seq, seq, xp_ref, x_ref, xn_ref, nw_ref, wc_ref, wz_ref, wdt_ref,
                 dww_ref, dwb_ref, lng_ref, lnb_ref, cw_ref, cb_ref, dtb_ref,
                 g_ref, xbc_ref, z_ref, dt_ref, h_ref, ge_ref, xe_ref, co_ref):
    ext = ROW_TILE + 2 * HALO
    pos0 = (pl.program_id(0) % tiles_per_seq) * ROW_TILE
    nw = nw_ref[...]
    h_ref[0:HALO, :] = _rms(xp_ref[...], nw).astype(MM_DTYPE)
    h_ref[HALO:HALO + ROW_TILE, :] = _rms(x_ref[...], nw).astype(MM_DTYPE)
    h_ref[HALO + ROW_TILE:, :] = _rms(xn_ref[...], nw).astype(MM_DTYPE)
    pos = pos0 - HALO + lax.broadcasted_iota(jnp.int32, (ext, 1), 0)
    inside = ((pos >= 0) & (pos < seq)).astype(F32)
    h_ext = h_ref[...]
    h_main = h_ref[HALO:HALO + ROW_TILE, :]

    glu_a = _mm(h_ext, wc_ref[:, :CONV_CH])
    glu_b = _mm(h_ext, wc_ref[:, CONV_CH:2 * CONV_CH])
    ge_ref[...] = glu_a * jax.nn.sigmoid(glu_b) * inside
    for j in range(SSM_XBC // CONV_CH):
        lo = 2 * CONV_CH + j * CONV_CH
        xe_ref[:, j * CONV_CH:(j + 1) * CONV_CH] = _mm(h_ext, wc_ref[:, lo:lo + CONV_CH]) * inside
    z_ref[...] = _mm(h_main, wz_ref[...]).astype(z_ref.dtype)
    dt_ref[...] = jax.nn.softplus(_mm(h_main, wdt_ref[...]) + dtb_ref[...])

    def conv_rows(i, carry):
        r0 = pl.multiple_of(i * CONV_ROWS, CONV_ROWS)
        for j in range(CONV_CH // LANES):
            cols = slice(j * LANES, (j + 1) * LANES)
            win = ge_ref[pl.ds(r0, CONV_ROWS + 2 * HALO), cols]
            acc = jnp.broadcast_to(dwb_ref[:, cols], (CONV_ROWS, LANES))
            for r in range(SUBLANES):
                sh = _rows_up(win, r)
                for k in range(CONV_K):
                    lo = HALO - CONV_K // 2 + k
                    if lo % SUBLANES == r:
                        acc = acc + sh[lo - r:lo - r + CONV_ROWS, :] * dww_ref[k:k + 1, cols]
            co_ref[pl.ds(r0, CONV_ROWS), cols] = acc
        return carry

    lax.fori_loop(0, ROW_TILE // CONV_ROWS, conv_rows, 0)
    conv = co_ref[...]
    mu = jnp.mean(conv, axis=-1, keepdims=True)
    xc = conv - mu
    y = xc * lax.rsqrt(jnp.mean(xc * xc, axis=-1, keepdims=True) + NORM_EPS) * lng_ref[...] + lnb_ref[...]
    g_ref[...] = (y * jax.nn.sigmoid(y)).astype(g_ref.dtype)

    def ssm_rows(i, carry):
        r0 = pl.multiple_of(i * SSM_CONV_ROWS, SSM_CONV_ROWS)
        for j in range(SSM_XBC // SSM_CONV_LANES):
            cols = slice(j * SSM_CONV_LANES, (j + 1) * SSM_CONV_LANES)
            win = xe_ref[pl.ds(r0 + HALO - SUBLANES, SSM_CONV_ROWS + 2 * SUBLANES), cols]
            acc = jnp.broadcast_to(cb_ref[:, cols], (SSM_CONV_ROWS, SSM_CONV_LANES))
            for k in range(SSM_CONV):
                lo = SUBLANES - SSM_CONV // 2 + k
                r = lo % SUBLANES
                acc = acc + _rows_up(win, r)[lo - r:lo - r + SSM_CONV_ROWS, :] * cw_ref[k:k + 1, cols]
            xbc_ref[pl.ds(r0, SSM_CONV_ROWS), cols] = (acc * jax.nn.sigmoid(acc)).astype(xbc_ref.dtype)
        return carry

    lax.fori_loop(0, ROW_TILE // SSM_CONV_ROWS, ssm_rows, 0)


def _odd_in(x, nw, w_conv, w_z, w_dt, dw_w, dw_b, ln_g, ln_b, conv_w, conv_b, dt_bias, seq):
    rows = x.shape[0]
    tiles_per_seq = seq // ROW_TILE
    per_tile = ROW_TILE // HALO
    last = rows // HALO - 1
    x_prev = pl.BlockSpec((HALO, D_MODEL), lambda i: (jnp.maximum(i * per_tile - 1, 0), 0))
    x_next = pl.BlockSpec((HALO, D_MODEL), lambda i: (jnp.minimum((i + 1) * per_tile, last), 0))
    ext = ROW_TILE + 2 * HALO
    return pl.pallas_call(
        functools.partial(_odd_in_body, tiles_per_seq, seq),
        grid=(rows // ROW_TILE,),
        in_specs=[x_prev, _rows(D_MODEL), x_next, _resident((1, D_MODEL)),
                  _resident((D_MODEL, 2 * CONV_CH + SSM_XBC)), _resident((D_MODEL, SSM_INNER)),
                  _resident((D_MODEL, LANES)),
                  _resident((CONV_K, CONV_CH)), _resident((1, CONV_CH)), _resident((1, CONV_CH)),
                  _resident((1, CONV_CH)), _resident((SSM_CONV, SSM_XBC)), _resident((1, SSM_XBC)),
                  _resident((1, LANES))],
        out_specs=[_rows(CONV_CH), _rows(SSM_XBC), _rows(SSM_INNER), _rows(LANES)],
        out_shape=[jax.ShapeDtypeStruct((rows, CONV_CH), ACT_DTYPE),
                   jax.ShapeDtypeStruct((rows, SSM_XBC), ACT_DTYPE),
                   jax.ShapeDtypeStruct((rows, SSM_INNER), F32),
                   jax.ShapeDtypeStruct((rows, LANES), F32)],
        scratch_shapes=[pltpu.VMEM((ext, D_MODEL), MM_DTYPE),
                        pltpu.VMEM((ext, CONV_CH), F32),
                        pltpu.VMEM((ext, SSM_XBC), F32),
                        pltpu.VMEM((ROW_TILE, CONV_CH), F32)],
        compiler_params=_params("parallel"),
        name="odd_in",
    )(x, x, x, nw, w_conv, w_z, w_dt, dw_w, dw_b, ln_g, ln_b, conv_w, conv_b, dt_bias)


def _pair_cols(v, h0):
    lane = lax.broadcasted_iota(jnp.int32, (SSM_CHUNK, LANES), 1)
    return jnp.where(lane < SSM_HEAD_DIM, v[:, h0:h0 + 1], v[:, h0 + 1:h0 + 2])


def _ssd_direction(xbc, dt, alog_row, alog_col, col0, state_ref, reverse):
    xs = xbc[:, :SSM_INNER].astype(F32)
    bm = xbc[:, SSM_INNER:SSM_INNER + SSM_GN]
    cm = xbc[:, SSM_INNER + SSM_GN:]
    li = lax.broadcasted_iota(jnp.int32, (SSM_CHUNK, SSM_CHUNK), 0)
    si = lax.broadcasted_iota(jnp.int32, (SSM_CHUNK, SSM_CHUNK), 1)
    if reverse:
        keep = si >= li
        sum_rows = (si >= li).astype(F32)
        sum_lanes = (li >= si).astype(F32)
        edge = 0
    else:
        keep = si <= li
        sum_rows = (si <= li).astype(F32)
        sum_lanes = (li <= si).astype(F32)
        edge = SSM_CHUNK - 1
    a_col = dt * (-jnp.exp(alog_row))
    a_row = dt.T * (-jnp.exp(alog_col))
    acs_col = _mm_exact(sum_rows, a_col)
    acs_row = _mm_exact(a_row, sum_lanes)
    total = acs_col[edge:edge + 1, :]
    in_decay = jnp.exp(acs_col)
    out_decay = jnp.exp(total - acs_col)
    chunk_decay = jnp.exp(total)

    y_parts = []
    xd_parts = []
    for g in range(SSM_GROUPS):
        bg = bm[:, g * SSM_STATE:(g + 1) * SSM_STATE]
        cg = cm[:, g * SSM_STATE:(g + 1) * SSM_STATE]
        cb = _mm_nt(cg, bg)
        y_off = _mm(cg, state_ref[g].astype(MM_DTYPE))
        for pair in range(GROUP_LANES // LANES):
            h0 = g * (SSM_HEADS // SSM_GROUPS) + 2 * pair
            lo = h0 * SSM_HEAD_DIM
            x_pair = xs[:, lo:lo + LANES] * _pair_cols(dt, col0 + h0)
            xp = x_pair.astype(MM_DTYPE)
            halves = []
            for h in (h0, h0 + 1):
                seg = acs_col[:, col0 + h:col0 + h + 1] - acs_row[col0 + h:col0 + h + 1, :]
                lmat = jnp.exp(jnp.where(keep, seg, -jnp.inf))
                halves.append(_mm((cb * lmat).astype(MM_DTYPE), xp))
            lane = lax.broadcasted_iota(jnp.int32, (SSM_CHUNK, LANES), 1)
            y_diag = jnp.where(lane < SSM_HEAD_DIM, halves[0], halves[1])
            y_parts.append(y_diag + y_off[:, pair * LANES:(pair + 1) * LANES] * _pair_cols(in_decay, col0 + h0))
            xd_parts.append((x_pair * _pair_cols(out_decay, col0 + h0)).astype(MM_DTYPE))
        xd = jnp.concatenate(xd_parts[-(GROUP_LANES // LANES):], axis=1)
        new_state = _mm_tn(bg, xd)
        cd = jnp.concatenate(
            [jnp.broadcast_to(chunk_decay[:, col0 + g * 4 + j:col0 + g * 4 + j + 1], (1, SSM_HEAD_DIM))
             for j in range(SSM_HEADS // SSM_GROUPS)], axis=1)
        state_ref[g] = state_ref[g] * cd + new_state
    return jnp.concatenate(y_parts, axis=1)


def _ssd_body(nchunks, xf_ref, xb_ref, dtf_ref, dtb_ref, alr_ref, alc_ref, yf_ref, yb_ref, sf_ref, sb_ref):
    @pl.when(pl.program_id(1) == 0)
    def _():
        sf_ref[...] = jnp.zeros_like(sf_ref)
        sb_ref[...] = jnp.zeros_like(sb_ref)

    alr, alc = alr_ref[...], alc_ref[...]
    yf_ref[...] = _ssd_direction(xf_ref[...], dtf_ref[...], alr, alc, 0, sf_ref, False)
    yb_ref[...] = _ssd_direction(xb_ref[...], dtb_ref[...], alr, alc, SSM_HEADS, sb_ref, True)


def _ssd(xbc, dt, alog_row, alog_col, batch, seq):
    rows = xbc.shape[0]
    nchunks = seq // SSM_CHUNK
    fwd = lambda w: pl.BlockSpec((SSM_CHUNK, w), lambda b, i: (b * nchunks + i, 0))
    bwd = lambda w: pl.BlockSpec((SSM_CHUNK, w), lambda b, i: (b * nchunks + nchunks - 1 - i, 0))
    state = pltpu.VMEM((SSM_GROUPS, SSM_STATE, GROUP_LANES), F32)
    return pl.pallas_call(
        functools.partial(_ssd_body, nchunks),
        grid=(batch, nchunks),
        in_specs=[fwd(SSM_XBC), bwd(SSM_XBC), fwd(LANES), bwd(LANES),
                  pl.BlockSpec((1, LANES), lambda b, i: (0, 0)), pl.BlockSpec((LANES, 1), lambda b, i: (0, 0))],
        out_specs=[fwd(SSM_INNER), bwd(SSM_INNER)],
        out_shape=[jax.ShapeDtypeStruct((rows, SSM_INNER), F32), jax.ShapeDtypeStruct((rows, SSM_INNER), F32)],
        scratch_shapes=[state, state],
        compiler_params=_params("parallel", "arbitrary"),
        name="ssd",
    )(xbc, xbc, dt, dt, alog_row, alog_col)


def _odd_out_body(x_ref, g_ref, xs_ref, yf_ref, yb_ref, z_ref, dskip_ref, gnw_ref, wo_ref, nw_ref, o_ref):
    z = z_ref[...].astype(F32)
    y = yf_ref[...] + yb_ref[...] + xs_ref[...].astype(F32) * dskip_ref[...]
    y = _rms(y * (z * jax.nn.sigmoid(z)), gnw_ref[...]).astype(MM_DTYPE)
    m = _mm(g_ref[...], wo_ref[:CONV_CH, :]) + _mm(y, wo_ref[CONV_CH:, :])
    o_ref[...] = x_ref[...] + _rms(m, nw_ref[...])


def _odd_out(x, g, xbc, y_f, y_b, z, d_skip, gn_w, w_out, nw):
    rows = x.shape[0]
    return pl.pallas_call(
        _odd_out_body,
        grid=(rows // ROW_TILE,),
        in_specs=[_rows(D_MODEL), _rows(CONV_CH), _rows(SSM_INNER), _rows(SSM_INNER), _rows(SSM_INNER),
                  _rows(SSM_INNER), _resident((1, SSM_INNER)), _resident((1, SSM_INNER)),
                  _resident((CONV_CH + SSM_INNER, D_MODEL)), _resident((1, D_MODEL))],
        out_specs=_rows(D_MODEL),
        out_shape=jax.ShapeDtypeStruct((rows, D_MODEL), F32),
        compiler_params=_params("parallel"),
        name="odd_out",
    )(x, g, xbc, y_f, y_b, z, d_skip, gn_w, w_out, nw)


def _odd_mixer(x, nw_pre, nw_post, w_in, dw_w, dw_b, ln_g, ln_b, conv_w, conv_b, a_log, dt_bias, d_skip,
               gn_w, w_out, batch, seq):
    n_conv = 2 * CONV_CH
    off_z, off_xbc, off_dt = n_conv, n_conv + SSM_INNER, n_conv + SSM_INNER + SSM_XBC
    w_conv = jnp.concatenate([w_in[:, :n_conv], w_in[:, off_xbc:off_dt]], axis=1).astype(MM_DTYPE)
    w_z = w_in[:, off_z:off_xbc].astype(MM_DTYPE)
    n_dt = 2 * SSM_HEADS
    w_dt = jnp.pad(w_in[:, off_dt:], ((0, 0), (0, LANES - n_dt))).astype(MM_DTYPE)
    dtb = jnp.pad(dt_bias.reshape(1, n_dt), ((0, 0), (0, LANES - n_dt)))
    alog = jnp.pad(a_log.reshape(1, n_dt), ((0, 0), (0, LANES - n_dt)))
    g, xbc, z, dt = _odd_in(x, nw_pre, w_conv, w_z, w_dt, dw_w, dw_b[None, :], ln_g[None, :], ln_b[None, :],
                            conv_w, conv_b[None, :], dtb, seq)
    y_f, y_b = _ssd(xbc, dt, alog, alog.reshape(LANES, 1), batch, seq)
    d_row = jnp.repeat(d_skip, SSM_HEAD_DIM)[None, :]
    return _odd_out(x, g, xbc, y_f, y_b, z, d_row, gn_w[None, :], w_out.astype(MM_DTYPE), nw_post)


def kernel(x, norm_w, ffn_w_gate, ffn_w_up, ffn_w_down, ev_w_in, ev_pool_w, ev_pool_scale, ev_sink, ev_w_out,
           od_w_in, cv_dw_w, cv_dw_b, cv_ln_g, cv_ln_b, ssm_conv_w, ssm_conv_b, ssm_A_log, ssm_dt_bias, ssm_D,
           ssm_norm_w, od_w_out):
    batch, seq, d = x.shape
    assert d == D_MODEL and seq % ROW_TILE == 0
    h = x.reshape(batch * seq, d)
    wg, wu, wd = (w.astype(MM_DTYPE) for w in (ffn_w_gate, ffn_w_up, ffn_w_down))
    for l in range(DEPTH):
        nw = norm_w[l][:, None, :]
        i = l // 2
        h = _ffn(h, nw[0], nw[1], wg[l, 0], wu[l, 0], wd[l, 0])
        if l % 2 == 0:
            h = _even_mixer(h, nw[2], nw[3], ev_w_in[i], ev_pool_w[i], ev_pool_scale[i], ev_sink[i],
                            ev_w_out[i], seq)
        else:
            h = _odd_mixer(h, nw[2], nw[3], od_w_in[i], cv_dw_w[i], cv_dw_b[i], cv_ln_g[i], cv_ln_b[i],
                           ssm_conv_w[i], ssm_conv_b[i], ssm_A_log[i], ssm_dt_bias[i], ssm_D[i],
                           ssm_norm_w[i], od_w_out[i], batch, seq)
        h = _ffn(h, nw[4], nw[5], wg[l, 1], wu[l, 1], wd[l, 1])
    return h.reshape(batch, seq, d)
```

```python
import functools

import jax
import jax.numpy as jnp
from jax import lax
from jax.experimental import pallas as pl
from jax.experimental.pallas import tpu as pltpu

F32 = jnp.float32
MM_DTYPE = jnp.bfloat16
ACT_DTYPE = jnp.bfloat16

D_MODEL = 1024
DEPTH = 4
HEAD_DIM = 64
NORM_EPS = 1e-6
D_FF = 2816

POOL_WINDOWS = (2, 4, 8, 16)
POOL_WIDTH = 256
POOL_GC = 64
ATTN_Q_HEADS = 12
ATTN_KV_HEADS = 4
GQA_GROUP = 3
WINDOW = 128
ATTN_BLOCK = 128
ROPE_THETA = 10000.0
Q_WIDTH = ATTN_Q_HEADS * HEAD_DIM
KV_WIDTH = ATTN_KV_HEADS * HEAD_DIM
EVEN_IN = POOL_WIDTH + Q_WIDTH + 2 * KV_WIDTH

CONV_CH = 512
CONV_K = 31
SSM_INNER = 1024
SSM_HEAD_DIM = 64
SSM_HEADS = 16
SSM_GROUPS = 4
SSM_STATE = 128
SSM_CONV = 4
SSM_CHUNK = 128
SSM_GN = SSM_GROUPS * SSM_STATE
SSM_XBC = SSM_INNER + 2 * SSM_GN
GROUP_LANES = (SSM_HEADS // SSM_GROUPS) * SSM_HEAD_DIM

LANES = 128
SUBLANES = 8
ROW_TILE = 512
HALO = 16
POOL_HALO = 8
ATTN_TILE = 256
SSD_STEP_CHUNKS = 2
FFN_TILE = 1024
FFN_PART = 512
FF_CHUNK = 256
CONV_ROWS = 128
SSM_CONV_ROWS = 64
SSM_CONV_LANES = 256
VMEM_LIMIT_BYTES = 56 * 1024 * 1024


def _params(*semantics):
    return pltpu.CompilerParams(dimension_semantics=semantics, vmem_limit_bytes=VMEM_LIMIT_BYTES)


def _rms(x, w):
    return x * lax.rsqrt(jnp.mean(x * x, axis=-1, keepdims=True) + NORM_EPS) * w


def _mm(a, b):
    return jnp.dot(a, b, preferred_element_type=F32)


def _mm_nt(a, b):
    return lax.dot_general(a, b, (((1,), (1,)), ((), ())), preferred_element_type=F32)


def _mm_tn(a, b):
    return lax.dot_general(a, b, (((0,), (0,)), ((), ())), preferred_element_type=F32)


def _mm_exact(a, b):
    return jnp.dot(a, b, preferred_element_type=F32, precision=lax.Precision.HIGHEST)


def _rows_up(v, r):
    return v if r == 0 else pltpu.roll(v, v.shape[0] - r, 0)


def _resident(shape):
    nd = len(shape)
    return pl.BlockSpec(shape, lambda *_: (0,) * nd, pipeline_mode=pl.Buffered(1))


def _rows(width, tile=ROW_TILE):
    return pl.BlockSpec((tile, width), lambda i: (i, 0))


def _ffn_body(x_ref, wpre_ref, wpost_ref, wg_ref, wu_ref, wd_ref, o_ref, acc_ref):
    parts = [slice(i * FFN_PART, (i + 1) * FFN_PART) for i in range(FFN_TILE // FFN_PART)]
    hs = [_rms(x_ref[r, :], wpre_ref[...]).astype(MM_DTYPE) for r in parts]
    for r, h in zip(parts, hs):
        for c in range(D_FF // FF_CHUNK):
            cols = slice(c * FF_CHUNK, (c + 1) * FF_CHUNK)
            g = _mm(h, wg_ref[:, cols])
            u = _mm(h, wu_ref[:, cols])
            a = (g * jax.nn.sigmoid(g) * u).astype(MM_DTYPE)
            part = _mm(a, wd_ref[cols, :])
            if c == 0:
                acc_ref[r, :] = part
            else:
                acc_ref[r, :] += part
        o_ref[r, :] = x_ref[r, :] + 0.5 * _rms(acc_ref[r, :], wpost_ref[...])


def _ffn(x, w_pre, w_post, w_gate, w_up, w_down):
    rows = x.shape[0]
    return pl.pallas_call(
        _ffn_body,
        grid=(rows // FFN_TILE,),
        in_specs=[_rows(D_MODEL, FFN_TILE), _resident((1, D_MODEL)), _resident((1, D_MODEL)),
                  _resident((D_MODEL, D_FF)), _resident((D_MODEL, D_FF)), _resident((D_FF, D_MODEL))],
        out_specs=_rows(D_MODEL, FFN_TILE),
        out_shape=jax.ShapeDtypeStruct((rows, D_MODEL), F32),
        scratch_shapes=[pltpu.VMEM((FFN_TILE, D_MODEL), F32)],
        compiler_params=_params("parallel"),
        name="ffn",
    )(x, w_pre, w_post, w_gate, w_up, w_down)


def _even_in_body(x_ref, nw_ref, w_ref, cos_ref, sina_ref, sinb_ref, u_ref, q_ref, k_ref, v_ref):
    h = _rms(x_ref[...], nw_ref[...]).astype(MM_DTYPE)
    u_ref[...] = _mm(h, w_ref[:, :POOL_WIDTH])
    cos, sin_a, sin_b = cos_ref[...], sina_ref[...], sinb_ref[...]
    qk_width = Q_WIDTH + KV_WIDTH
    for j in range(qk_width // LANES):
        lo = POOL_WIDTH + j * LANES
        t = _mm(h, w_ref[:, lo:lo + LANES])
        r = t * cos + pltpu.roll(t, LANES - HEAD_DIM // 2, 1) * sin_a + pltpu.roll(t, HEAD_DIM // 2, 1) * sin_b
        if j * LANES < Q_WIDTH:
            q_ref[:, j * LANES:(j + 1) * LANES] = (r * (HEAD_DIM ** -0.5)).astype(q_ref.dtype)
        else:
            kc = j * LANES - Q_WIDTH
            k_ref[:, kc:kc + LANES] = r.astype(k_ref.dtype)
    v_ref[...] = _mm(h, w_ref[:, POOL_WIDTH + qk_width:]).astype(v_ref.dtype)


def _even_in(x, nw, w_in, cos, sin_a, sin_b, seq):
    rows = x.shape[0]
    tiles_per_seq = seq // ROW_TILE
    table = pl.BlockSpec((ROW_TILE, LANES), lambda i: (i % tiles_per_seq, 0))
    return pl.pallas_call(
        _even_in_body,
        grid=(rows // ROW_TILE,),
        in_specs=[_rows(D_MODEL), _resident((1, D_MODEL)), _resident((D_MODEL, EVEN_IN)), table, table, table],
        out_specs=[_rows(POOL_WIDTH), _rows(Q_WIDTH), _rows(KV_WIDTH), _rows(KV_WIDTH)],
        out_shape=[jax.ShapeDtypeStruct((rows, POOL_WIDTH), F32),
                   jax.ShapeDtypeStruct((rows, Q_WIDTH), ACT_DTYPE),
                   jax.ShapeDtypeStruct((rows, KV_WIDTH), ACT_DTYPE),
                   jax.ShapeDtypeStruct((rows, KV_WIDTH), ACT_DTYPE)],
        compiler_params=_params("parallel"),
        name="even_in",
    )(x, nw, w_in, cos, sin_a, sin_b)


def _attn_body(blocks_per_seq, sink_ref, q_ref, kp_ref, k_ref, kn_ref, vp_ref, v_ref, vn_ref, o_ref):
    blk0 = (pl.program_id(0) * (ATTN_TILE // ATTN_BLOCK)) % blocks_per_seq
    keys = jnp.concatenate([kp_ref[...], k_ref[...], kn_ref[...]], axis=0)
    vals = jnp.concatenate([vp_ref[...], v_ref[...], vn_ref[...]], axis=0)
    vals_t = vals.astype(F32).T.astype(MM_DTYPE)
    nwin = 3 * ATTN_BLOCK
    nq = GQA_GROUP * ATTN_BLOCK
    head_of_lane = lax.broadcasted_iota(jnp.int32, (nq, KV_WIDTH), 1) // HEAD_DIM
    key_i = lax.broadcasted_iota(jnp.int32, (nwin, ATTN_BLOCK), 0)
    qry_i = lax.broadcasted_iota(jnp.int32, (nwin, ATTN_BLOCK), 1)
    band = (key_i - qry_i >= 0) & (key_i - qry_i <= 2 * WINDOW)
    for jb in range(ATTN_TILE // ATTN_BLOCK):
        blk = blk0 + jb
        rows = slice(jb * ATTN_BLOCK, (jb + 1) * ATTN_BLOCK)
        win = slice(jb * ATTN_BLOCK, jb * ATTN_BLOCK + nwin)
        valid = band & ((key_i >= ATTN_BLOCK) | (blk > 0)) & ((key_i < 2 * ATTN_BLOCK) | (blk < blocks_per_seq - 1))
        bias = jnp.where(valid, 0.0, -jnp.inf)
        bias = jnp.concatenate([bias] * GQA_GROUP, axis=1)
        qs = jnp.concatenate([q_ref[rows, g * KV_WIDTH:(g + 1) * KV_WIDTH] for g in range(GQA_GROUP)], axis=0)
        qm = jnp.concatenate([jnp.where(head_of_lane == kv, qs, jnp.zeros_like(qs))
                              for kv in range(ATTN_KV_HEADS)], axis=0)
        scores = _mm_nt(keys[win], qm)
        outs = []
        for kv in range(ATTN_KV_HEADS):
            s = scores[:, kv * nq:(kv + 1) * nq] + bias
            sink = jnp.concatenate([jnp.full((1, ATTN_BLOCK), sink_ref[kv * GQA_GROUP + g], F32)
                                    for g in range(GQA_GROUP)], axis=1)
            m = jnp.maximum(jnp.max(s, axis=0, keepdims=True), sink)
            p = jnp.exp(s - m)
            denom = jnp.sum(p, axis=0, keepdims=True) + jnp.exp(sink - m)
            pn = (p * (1.0 / denom)).astype(MM_DTYPE)
            outs.append(_mm(vals_t[kv * HEAD_DIM:(kv + 1) * HEAD_DIM, win], pn))
        out = jnp.concatenate(outs, axis=0).T
        for g in range(GQA_GROUP):
            o_ref[rows, g * KV_WIDTH:(g + 1) * KV_WIDTH] = out[g * ATTN_BLOCK:(g + 1) * ATTN_BLOCK].astype(o_ref.dtype)


def _attn(q, k, v, sink, seq):
    rows = q.shape[0]
    per_tile = ATTN_TILE // ATTN_BLOCK
    last = rows // ATTN_BLOCK - 1
    blocks_per_seq = seq // ATTN_BLOCK
    cur = lambda w: pl.BlockSpec((ATTN_TILE, w), lambda i: (i, 0))
    prev = pl.BlockSpec((ATTN_BLOCK, KV_WIDTH), lambda i: (jnp.maximum(i * per_tile - 1, 0), 0))
    nxt = pl.BlockSpec((ATTN_BLOCK, KV_WIDTH), lambda i: (jnp.minimum((i + 1) * per_tile, last), 0))
    return pl.pallas_call(
        functools.partial(_attn_body, blocks_per_seq),
        grid=(rows // ATTN_TILE,),
        in_specs=[pl.BlockSpec(memory_space=pltpu.SMEM), cur(Q_WIDTH),
                  prev, cur(KV_WIDTH), nxt, prev, cur(KV_WIDTH), nxt],
        out_specs=cur(Q_WIDTH),
        out_shape=jax.ShapeDtypeStruct((rows, Q_WIDTH), ACT_DTYPE),
        compiler_params=_params("parallel"),
        name="attn",
    )(sink, q, k, k, k, v, v, v)


def _even_out_body(tiles_per_seq, seq, x_ref, up_ref, u_ref, un_ref, a_ref, tap_ref, hw_ref, pw_ref, ps_ref,
                   wo_ref, nw_ref, o_ref, ue_ref):
    pos0 = (pl.program_id(0) % tiles_per_seq) * ROW_TILE
    ue_ref[0:POOL_HALO, :] = jnp.where(pos0 > 0, up_ref[...], 0.0)
    ue_ref[POOL_HALO:POOL_HALO + ROW_TILE, :] = u_ref[...]
    ue_ref[POOL_HALO + ROW_TILE:, :] = jnp.where(pos0 + ROW_TILE < seq, un_ref[...], 0.0)
    acc = jnp.zeros((ROW_TILE, POOL_WIDTH), F32)
    for d in range(2 * POOL_HALO):
        acc = acc + ue_ref[d:d + ROW_TILE, :] * tap_ref[d:d + 1, :]
    pos = pos0 + lax.broadcasted_iota(jnp.int32, (ROW_TILE, POOL_WIDTH), 0)
    hw = hw_ref[...]
    cnt = (jnp.minimum(pos + hw, seq) - jnp.maximum(pos - hw, 0)).astype(F32)
    p = (acc / cnt - u_ref[...]).astype(MM_DTYPE)
    y = (_mm(p, pw_ref[...]) * ps_ref[...]).astype(MM_DTYPE)
    m = _mm(y, wo_ref[:POOL_WIDTH, :]) + _mm(a_ref[...], wo_ref[POOL_WIDTH:, :])
    o_ref[...] = x_ref[...] + _rms(m, nw_ref[...])


def _even_out(x, u, attn, taps, half_w, pool_w, pool_scale, w_out, nw, seq):
    rows = x.shape[0]
    tiles_per_seq = seq // ROW_TILE
    per_tile = ROW_TILE // POOL_HALO
    last = rows // POOL_HALO - 1
    u_prev = pl.BlockSpec((POOL_HALO, POOL_WIDTH), lambda i: (jnp.maximum(i * per_tile - 1, 0), 0))
    u_next = pl.BlockSpec((POOL_HALO, POOL_WIDTH), lambda i: (jnp.minimum((i + 1) * per_tile, last), 0))
    return pl.pallas_call(
        functools.partial(_even_out_body, tiles_per_seq, seq),
        grid=(rows // ROW_TILE,),
        in_specs=[_rows(D_MODEL), u_prev, _rows(POOL_WIDTH), u_next, _rows(Q_WIDTH),
                  _resident((2 * POOL_HALO, POOL_WIDTH)), _resident((1, POOL_WIDTH)),
                  _resident((POOL_WIDTH, POOL_WIDTH)), _resident((1, POOL_WIDTH)),
                  _resident((POOL_WIDTH + Q_WIDTH, D_MODEL)), _resident((1, D_MODEL))],
        out_specs=_rows(D_MODEL),
        out_shape=jax.ShapeDtypeStruct((rows, D_MODEL), F32),
        scratch_shapes=[pltpu.VMEM((ROW_TILE + 2 * POOL_HALO, POOL_WIDTH), F32)],
        compiler_params=_params("parallel"),
        name="even_out",
    )(x, u, u, u, attn, taps, half_w, pool_w, pool_scale, w_out, nw)


def _rope_tables(seq):
    inv = 1.0 / (ROPE_THETA ** (jnp.arange(0, HEAD_DIM, 2, dtype=F32) / HEAD_DIM))
    ang = jnp.arange(seq, dtype=F32)[:, None] * inv[None, :]
    cos, sin = jnp.cos(ang), jnp.sin(ang)
    zero = jnp.zeros_like(sin)
    reps = LANES // HEAD_DIM
    cos_t = jnp.tile(jnp.concatenate([cos, cos], axis=1), (1, reps))
    sin_a = jnp.tile(jnp.concatenate([-sin, zero], axis=1), (1, reps))
    sin_b = jnp.tile(jnp.concatenate([zero, sin], axis=1), (1, reps))
    return cos_t, sin_a, sin_b


def _pool_constants():
    offs = jnp.arange(2 * POOL_HALO)[:, None] - POOL_HALO
    half = jnp.repeat(jnp.array([w // 2 for w in POOL_WINDOWS], jnp.int32), POOL_GC)[None, :]
    taps = ((offs >= -half) & (offs <= half - 1)).astype(F32)
    return taps, half


def _even_mixer(x, nw_pre, nw_post, w_in, pool_w, pool_scale, sink, w_out, seq):
    cos, sin_a, sin_b = _rope_tables(seq)
    taps, half_w = _pool_constants()
    head_order = jnp.array([kv * GQA_GROUP + g for g in range(GQA_GROUP) for kv in range(ATTN_KV_HEADS)])
    q_cols = (head_order[:, None] * HEAD_DIM + jnp.arange(HEAD_DIM)[None, :]).reshape(-1)
    cols = jnp.concatenate([jnp.arange(POOL_WIDTH), POOL_WIDTH + q_cols, jnp.arange(POOL_WIDTH + Q_WIDTH, EVEN_IN)])
    u, q, k, v = _even_in(x, nw_pre, w_in[:, cols].astype(MM_DTYPE), cos, sin_a, sin_b, seq)
    attn = _attn(q, k, v, sink, seq)
    pool_bd = jax.scipy.linalg.block_diag(*[pool_w[g] for g in range(len(POOL_WINDOWS))]).astype(MM_DTYPE)
    w_out_rows = jnp.concatenate([jnp.arange(POOL_WIDTH), POOL_WIDTH + q_cols])
    return _even_out(x, u, attn, taps, half_w, pool_bd, pool_scale[None, :], w_out[w_out_rows].astype(MM_DTYPE),
                     nw_post, seq)


def _odd_in_body(tiles_per_seq, seq, xp_ref, x_ref, xn_ref, nw_ref, wc_ref, wz_ref, wdt_ref,
                 dww_ref, dwb_ref, lng_ref, lnb_ref, cw_ref, cb_ref, dtb_ref,
                 g_ref, xbc_ref, z_ref, dt_ref, h_ref, ge_ref, xe_ref, co_ref):
    ext = ROW_TILE + 2 * HALO
    pos0 = (pl.program_id(0) % tiles_per_seq) * ROW_TILE
    nw = nw_ref[...]
    h_ref[0:HALO, :] = _rms(xp_ref[...], nw).astype(MM_DTYPE)
    h_ref[HALO:HALO + ROW_TILE, :] = _rms(x_ref[...], nw).astype(MM_DTYPE)
    h_ref[HALO + ROW_TILE:, :] = _rms(xn_ref[...], nw).astype(MM_DTYPE)
    pos = pos0 - HALO + lax.broadcasted_iota(jnp.int32, (ext, 1), 0)
    inside = ((pos >= 0) & (pos < seq)).astype(F32)
    h_ext = h_ref[...]
    h_main = h_ref[HALO:HALO + ROW_TILE, :]

    glu_a = _mm(h_ext, wc_ref[:, :CONV_CH])
    glu_b = _mm(h_ext, wc_ref[:, CONV_CH:2 * CONV_CH])
    ge_ref[...] = glu_a * jax.nn.sigmoid(glu_b) * inside
    for j in range(SSM_XBC // CONV_CH):
        lo = 2 * CONV_CH + j * CONV_CH
        xe_ref[:, j * CONV_CH:(j + 1) * CONV_CH] = _mm(h_ext, wc_ref[:, lo:lo + CONV_CH]) * inside
    z_ref[...] = _mm(h_main, wz_ref[...]).astype(z_ref.dtype)
    dt_ref[...] = jax.nn.softplus(_mm(h_main, wdt_ref[...]) + dtb_ref[...])

    def conv_rows(i, carry):
        r0 = i * CONV_ROWS
        for j in range(CONV_CH // LANES):
            cols = slice(j * LANES, (j + 1) * LANES)
            win = ge_ref[pl.ds(r0, CONV_ROWS + 2 * HALO), cols]
            acc = jnp.broadcast_to(dwb_ref[:, cols], (CONV_ROWS, LANES))
            for r in range(SUBLANES):
                sh = _rows_up(win, r)
                for k in range(CONV_K):
                    lo = HALO - CONV_K // 2 + k
                    if lo % SUBLANES == r:
                        acc = acc + sh[lo - r:lo - r + CONV_ROWS, :] * dww_ref[k:k + 1, cols]
            co_ref[pl.ds(r0, CONV_ROWS), cols] = acc
        return carry

    for i in range(ROW_TILE // CONV_ROWS):
        conv_rows(i, 0)
    conv = co_ref[...]
    mu = jnp.mean(conv, axis=-1, keepdims=True)
    xc = conv - mu
    y = xc * lax.rsqrt(jnp.mean(xc * xc, axis=-1, keepdims=True) + NORM_EPS) * lng_ref[...] + lnb_ref[...]
    g_ref[...] = (y * jax.nn.sigmoid(y)).astype(g_ref.dtype)

    def ssm_rows(i, carry):
        r0 = i * SSM_CONV_ROWS
        for j in range(SSM_XBC // SSM_CONV_LANES):
            cols = slice(j * SSM_CONV_LANES, (j + 1) * SSM_CONV_LANES)
            win = xe_ref[pl.ds(r0 + HALO - SUBLANES, SSM_CONV_ROWS + 2 * SUBLANES), cols]
            acc = jnp.broadcast_to(cb_ref[:, cols], (SSM_CONV_ROWS, SSM_CONV_LANES))
            for k in range(SSM_CONV):
                lo = SUBLANES - SSM_CONV // 2 + k
                r = lo % SUBLANES
                acc = acc + _rows_up(win, r)[lo - r:lo - r + SSM_CONV_ROWS, :] * cw_ref[k:k + 1, cols]
            xbc_ref[pl.ds(r0, SSM_CONV_ROWS), cols] = (acc * jax.nn.sigmoid(acc)).astype(xbc_ref.dtype)
        return carry

    for i in range(ROW_TILE // SSM_CONV_ROWS):
        ssm_rows(i, 0)


def _odd_in(x, nw, w_conv, w_z, w_dt, dw_w, dw_b, ln_g, ln_b, conv_w, conv_b, dt_bias, seq):
    rows = x.shape[0]
    tiles_per_seq = seq // ROW_TILE
    per_tile = ROW_TILE // HALO
    last = rows // HALO - 1
    x_prev = pl.BlockSpec((HALO, D_MODEL), lambda i: (jnp.maximum(i * per_tile - 1, 0), 0))
    x_next = pl.BlockSpec((HALO, D_MODEL), lambda i: (jnp.minimum((i + 1) * per_tile, last), 0))
    ext = ROW_TILE + 2 * HALO
    return pl.pallas_call(
        functools.partial(_odd_in_body, tiles_per_seq, seq),
        grid=(rows // ROW_TILE,),
        in_specs=[x_prev, _rows(D_MODEL), x_next, _resident((1, D_MODEL)),
                  _resident((D_MODEL, 2 * CONV_CH + SSM_XBC)), _resident((D_MODEL, SSM_INNER)),
                  _resident((D_MODEL, LANES)),
                  _resident((CONV_K, CONV_CH)), _resident((1, CONV_CH)), _resident((1, CONV_CH)),
                  _resident((1, CONV_CH)), _resident((SSM_CONV, SSM_XBC)), _resident((1, SSM_XBC)),
                  _resident((1, LANES))],
        out_specs=[_rows(CONV_CH), _rows(SSM_XBC), _rows(SSM_INNER), _rows(LANES)],
        out_shape=[jax.ShapeDtypeStruct((rows, CONV_CH), ACT_DTYPE),
                   jax.ShapeDtypeStruct((rows, SSM_XBC), ACT_DTYPE),
                   jax.ShapeDtypeStruct((rows, SSM_INNER), ACT_DTYPE),
                   jax.ShapeDtypeStruct((rows, LANES), F32)],
        scratch_shapes=[pltpu.VMEM((ext, D_MODEL), MM_DTYPE),
                        pltpu.VMEM((ext, CONV_CH), F32),
                        pltpu.VMEM((ext, SSM_XBC), F32),
                        pltpu.VMEM((ROW_TILE, CONV_CH), F32)],
        compiler_params=_params("parallel"),
        name="odd_in",
    )(x, x, x, nw, w_conv, w_z, w_dt, dw_w, dw_b, ln_g, ln_b, conv_w, conv_b, dt_bias)


def _split_bf16(v):
    hi = v.astype(MM_DTYPE)
    lo = (v - hi.astype(F32)).astype(MM_DTYPE)
    return jnp.concatenate([hi, lo], axis=1)


def _ssd_direction(xbc, dt, alog_row, alog_col, expand, col0, state_ref, reverse):
    xs = xbc[:, :SSM_INNER]
    bm = xbc[:, SSM_INNER:SSM_INNER + SSM_GN]
    cm = xbc[:, SSM_INNER + SSM_GN:]
    li = lax.broadcasted_iota(jnp.int32, (SSM_CHUNK, SSM_CHUNK), 0)
    si = lax.broadcasted_iota(jnp.int32, (SSM_CHUNK, SSM_CHUNK), 1)
    if reverse:
        keep = si >= li
        sum_rows = (si >= li).astype(F32)
        sum_lanes = (li >= si).astype(F32)
        edge = 0
    else:
        keep = si <= li
        sum_rows = (si <= li).astype(F32)
        sum_lanes = (li <= si).astype(F32)
        edge = SSM_CHUNK - 1
    dt_row = dt.T
    a_col = dt * (-jnp.exp(alog_row))
    a_row = dt_row * (-jnp.exp(alog_col))
    acs_col = _mm_exact(sum_rows, a_col)
    acs_row = _mm_exact(a_row, sum_lanes)
    total = acs_col[edge:edge + 1, :]
    in_decay = jnp.exp(acs_col)
    state_w = dt * jnp.exp(total - acs_col)
    chunk_decay = jnp.broadcast_to(jnp.exp(total), (2 * SUBLANES, LANES))
    spread = _mm(jnp.concatenate([_split_bf16(state_w), _split_bf16(in_decay), _split_bf16(chunk_decay)], axis=0),
                 expand)
    xd = (xs.astype(F32) * spread[:SSM_CHUNK]).astype(MM_DTYPE)
    in_decay_x = spread[SSM_CHUNK:2 * SSM_CHUNK]
    chunk_decay_x = spread[2 * SSM_CHUNK:2 * SSM_CHUNK + 1]
    heads_per_group = SSM_HEADS // SSM_GROUPS
    head_of_lane = lax.broadcasted_iota(jnp.int32, (SSM_CHUNK, GROUP_LANES), 1) // SSM_HEAD_DIM

    y_parts = []
    for g in range(SSM_GROUPS):
        lanes = slice(g * GROUP_LANES, (g + 1) * GROUP_LANES)
        bg = bm[:, g * SSM_STATE:(g + 1) * SSM_STATE]
        cg = cm[:, g * SSM_STATE:(g + 1) * SSM_STATE]
        cb = _mm_nt(cg, bg)
        mats = []
        for j in range(heads_per_group):
            h = col0 + g * heads_per_group + j
            seg = acs_col[:, h:h + 1] - acs_row[h:h + 1, :]
            lmat = jnp.exp(jnp.where(keep, seg, -jnp.inf))
            mats.append((cb * lmat * dt_row[h:h + 1, :]).astype(MM_DTYPE))
        xs_g = xs[:, lanes]
        xs_heads = jnp.concatenate([jnp.where(head_of_lane == j, xs_g, jnp.zeros_like(xs_g))
                                    for j in range(heads_per_group)], axis=0)
        y_diag = _mm(jnp.concatenate(mats, axis=1), xs_heads)
        y_off = _mm(cg, state_ref[g].astype(MM_DTYPE)) * in_decay_x[:, lanes]
        y_parts.append(y_diag + y_off)
        state_ref[g] = state_ref[g] * chunk_decay_x[:, lanes] + _mm_tn(bg, xd[:, lanes])
    return jnp.concatenate(y_parts, axis=1)


def _ssd_body(nchunks, xf_ref, xb_ref, dtf_ref, dtb_ref, alr_ref, alc_ref, ef_ref, eb_ref, yf_ref, yb_ref,
              sf_ref, sb_ref):
    @pl.when(pl.program_id(1) == 0)
    def _():
        sf_ref[...] = jnp.zeros_like(sf_ref)
        sb_ref[...] = jnp.zeros_like(sb_ref)

    alr, alc = alr_ref[...], alc_ref[...]
    for c in range(SSD_STEP_CHUNKS):
        rows = slice(c * SSM_CHUNK, (c + 1) * SSM_CHUNK)
        yf = _ssd_direction(xf_ref[rows, :], dtf_ref[rows, :], alr, alc, ef_ref[...], 0, sf_ref, False)
        yf_ref[rows, :] = yf.astype(yf_ref.dtype)
        c_rev = SSD_STEP_CHUNKS - 1 - c
        rows = slice(c_rev * SSM_CHUNK, (c_rev + 1) * SSM_CHUNK)
        yb = _ssd_direction(xb_ref[rows, :], dtb_ref[rows, :], alr, alc, eb_ref[...], SSM_HEADS, sb_ref, True)
        yb_ref[rows, :] = yb.astype(yb_ref.dtype)


def _spread_matrix(col0):
    k = jnp.arange(2 * LANES)[:, None] % LANES
    head = jnp.arange(SSM_INNER)[None, :] // SSM_HEAD_DIM
    return (k == col0 + head).astype(MM_DTYPE)


def _ssd(xbc, dt, alog_row, alog_col, batch, seq):
    rows = xbc.shape[0]
    step_rows = SSD_STEP_CHUNKS * SSM_CHUNK
    nchunks = seq // step_rows
    fwd = lambda w: pl.BlockSpec((step_rows, w), lambda b, i: (b * nchunks + i, 0))
    bwd = lambda w: pl.BlockSpec((step_rows, w), lambda b, i: (b * nchunks + nchunks - 1 - i, 0))
    const = lambda shape: pl.BlockSpec(shape, lambda b, i: (0, 0))
    state = pltpu.VMEM((SSM_GROUPS, SSM_STATE, GROUP_LANES), F32)
    y_shape = jax.ShapeDtypeStruct((rows, SSM_INNER), ACT_DTYPE)
    return pl.pallas_call(
        functools.partial(_ssd_body, nchunks),
        grid=(batch, nchunks),
        in_specs=[fwd(SSM_XBC), bwd(SSM_XBC), fwd(LANES), bwd(LANES), const((1, LANES)), const((LANES, 1)),
                  const((2 * LANES, SSM_INNER)), const((2 * LANES, SSM_INNER))],
        out_specs=[fwd(SSM_INNER), bwd(SSM_INNER)],
        out_shape=[y_shape, y_shape],
        scratch_shapes=[state, state],
        compiler_params=_params("parallel", "arbitrary"),
        name="ssd",
    )(xbc, xbc, dt, dt, alog_row, alog_col, _spread_matrix(0), _spread_matrix(SSM_HEADS))


def _odd_out_body(x_ref, g_ref, xs_ref, yf_ref, yb_ref, z_ref, dskip_ref, gnw_ref, wo_ref, nw_ref, o_ref):
    z = z_ref[...].astype(F32)
    y = yf_ref[...].astype(F32) + yb_ref[...].astype(F32) + xs_ref[...].astype(F32) * dskip_ref[...]
    y = _rms(y * (z * jax.nn.sigmoid(z)), gnw_ref[...]).astype(MM_DTYPE)
    m = _mm(g_ref[...], wo_ref[:CONV_CH, :]) + _mm(y, wo_ref[CONV_CH:, :])
    o_ref[...] = x_ref[...] + _rms(m, nw_ref[...])


def _odd_out(x, g, xbc, y_f, y_b, z, d_skip, gn_w, w_out, nw):
    rows = x.shape[0]
    return pl.pallas_call(
        _odd_out_body,
        grid=(rows // ROW_TILE,),
        in_specs=[_rows(D_MODEL), _rows(CONV_CH), _rows(SSM_INNER), _rows(SSM_INNER), _rows(SSM_INNER),
                  _rows(SSM_INNER), _resident((1, SSM_INNER)), _resident((1, SSM_INNER)),
                  _resident((CONV_CH + SSM_INNER, D_MODEL)), _resident((1, D_MODEL))],
        out_specs=_rows(D_MODEL),
        out_shape=jax.ShapeDtypeStruct((rows, D_MODEL), F32),
        compiler_params=_params("parallel"),
        name="odd_out",
    )(x, g, xbc, y_f, y_b, z, d_skip, gn_w, w_out, nw)


def _odd_mixer(x, nw_pre, nw_post, w_in, dw_w, dw_b, ln_g, ln_b, conv_w, conv_b, a_log, dt_bias, d_skip,
               gn_w, w_out, batch, seq):
    n_conv = 2 * CONV_CH
    off_z, off_xbc, off_dt = n_conv, n_conv + SSM_INNER, n_conv + SSM_INNER + SSM_XBC
    w_conv = jnp.concatenate([w_in[:, :n_conv], w_in[:, off_xbc:off_dt]], axis=1).astype(MM_DTYPE)
    w_z = w_in[:, off_z:off_xbc].astype(MM_DTYPE)
    n_dt = 2 * SSM_HEADS
    w_dt = jnp.pad(w_in[:, off_dt:], ((0, 0), (0, LANES - n_dt))).astype(MM_DTYPE)
    dtb = jnp.pad(dt_bias.reshape(1, n_dt), ((0, 0), (0, LANES - n_dt)))
    alog = jnp.pad(a_log.reshape(1, n_dt), ((0, 0), (0, LANES - n_dt)))
    g, xbc, z, dt = _odd_in(x, nw_pre, w_conv, w_z, w_dt, dw_w, dw_b[None, :], ln_g[None, :], ln_b[None, :],
                            conv_w, conv_b[None, :], dtb, seq)
    y_f, y_b = _ssd(xbc, dt, alog, alog.reshape(LANES, 1), batch, seq)
    d_row = jnp.repeat(d_skip, SSM_HEAD_DIM)[None, :]
    return _odd_out(x, g, xbc, y_f, y_b, z, d_row, gn_w[None, :], w_out.astype(MM_DTYPE), nw_post)


def kernel(x, norm_w, ffn_w_gate, ffn_w_up, ffn_w_down, ev_w_in, ev_pool_w, ev_pool_scale, ev_sink, ev_w_out,
           od_w_in, cv_dw_w, cv_dw_b, cv_ln_g, cv_ln_b, ssm_conv_w, ssm_conv_b, ssm_A_log, ssm_dt_bias, ssm_D,
           ssm_norm_w, od_w_out):
    batch, seq, d = x.shape
    assert d == D_MODEL and seq % ROW_TILE == 0
    h = x.reshape(batch * seq, d)
    wg, wu, wd = (w.astype(MM_DTYPE) for w in (ffn_w_gate, ffn_w_up, ffn_w_down))
    for l in range(DEPTH):
        nw = norm_w[l][:, None, :]
        i = l // 2
        h = _ffn(h, nw[0], nw[1], wg[l, 0], wu[l, 0], wd[l, 0])
        if l % 2 == 0:
            h = _even_mixer(h, nw[2], nw[3], ev_w_in[i], ev_pool_w[i], ev_pool_scale[i], ev_sink[i],
                            ev_w_out[i], seq)
        else:
            h = _odd_mixer(h, nw[2], nw[3], od_w_in[i], cv_dw_w[i], cv_dw_b[i], cv_ln_g[i], cv_ln_b[i],
                           ssm_conv_w[i], ssm_conv_b[i], ssm_A_log[i], ssm_dt_bias[i], ssm_D[i],
                           ssm_norm_w[i], od_w_out[i], batch, seq)
        h = _ffn(h, nw[4], nw[5], wg[l, 1], wu[l, 1], wd[l, 1])
    return h.reshape(batch, seq, d)
```
